```python
import math
import jax
import jax.numpy as jnp
from jax import lax
import numpy as np

D_MODEL = 1024
BATCH = 4
SEQ = 8192
DEPTH = 4

N_MIXERS = 4
HEAD_DIM = 64
N_HEADS = D_MODEL // HEAD_DIM
ROPE_THETA = 10000.0
RMS_EPS = 1e-6
NEG_INF = -1e30
ATTN_QBLOCK = 128
MOBA_BLOCK = 256
MOBA_TOPK = 3
MOBA_QCHUNK = 32
MLA_HEADS = N_HEADS
MLA_Q_LORA = 3 * D_MODEL // 8
MLA_KV_LORA = D_MODEL // 4
MLA_NOPE = HEAD_DIM
MLA_ROPE = HEAD_DIM // 2
MLA_V = HEAD_DIM
DIL_GROUPS = ((128, 1), (512, 4), (2048, 16))
DIL_HEADS = N_HEADS
DIL_BLOCK = 128
SB_HEADS = N_HEADS
SB_QBLOCK = 128
D_FF = ((8 * D_MODEL // 3 + 255) // 256) * 256
CONV_WIDTH = 3

kernel_name = "hybrid_moba_mla_dilated_stickbreak_convffn"


def rms_norm(x, g):
    xf = x.astype(jnp.float32)
    y = xf * lax.rsqrt(jnp.mean(xf * xf, axis=-1, keepdims=True) + RMS_EPS)
    return (y * g.astype(jnp.float32)).astype(x.dtype)


def rope_tables(pos, dim):
    inv = ROPE_THETA ** (-jnp.arange(0, dim, 2, dtype=jnp.float32) / dim)
    ang = pos[:, None] * inv[None, :]
    return jnp.cos(ang), jnp.sin(ang)


def apply_rope(x, cos, sin):
    half = x.shape[-1] // 2
    x1, x2 = x[..., :half], x[..., half:]
    c = cos[None, :, None, :].astype(x.dtype)
    s = sin[None, :, None, :].astype(x.dtype)
    return jnp.concatenate([x1 * c - x2 * s, x2 * c + x1 * s], axis=-1)


def moba_attention(q, k, v):
    B, S, H, dh = q.shape
    scale = dh ** -0.5
    nb = -(-S // MOBA_BLOCK)
    sp = nb * MOBA_BLOCK
    padw = ((0, 0), (0, sp - S), (0, 0), (0, 0))
    qh = jnp.pad(q, padw).transpose(0, 2, 1, 3)
    kh = jnp.pad(k, padw).transpose(0, 2, 1, 3)
    vh = jnp.pad(v, padw).transpose(0, 2, 1, 3)
    kblk = kh.reshape(B, H, nb, MOBA_BLOCK, dh)
    vblk = vh.reshape(B, H, nb, MOBA_BLOCK, dh)
    kmean = jnp.mean(kblk.astype(jnp.float32), axis=3)
    topk = min(MOBA_TOPK, nb)
    bi = jnp.arange(B)[:, None, None, None]
    hi = jnp.arange(H)[None, :, None, None]
    blk_ids = jnp.arange(nb)

    def step(c):
        start = c * MOBA_QCHUNK
        blk = start // MOBA_BLOCK
        qc = lax.dynamic_slice_in_dim(qh, start, MOBA_QCHUNK, axis=2)
        qpos = start + jnp.arange(MOBA_QCHUNK)
        gate = jnp.einsum("bhqd,bhnd->bhqn", qc.astype(jnp.float32), kmean)
        gate = jnp.where(blk_ids < blk, gate, NEG_INF)
        _, gidx = lax.top_k(gate, topk)
        valid = gidx < blk
        ksel = kblk[bi, hi, gidx]
        vsel = vblk[bi, hi, gidx]
        s_sel = jnp.einsum("bhqd,bhqnkd->bhqnk", qc, ksel, preferred_element_type=jnp.float32) * scale
        s_sel = jnp.where(valid[..., None], s_sel, NEG_INF)
        s_sel = s_sel.reshape(B, H, MOBA_QCHUNK, topk * MOBA_BLOCK)
        kown = lax.dynamic_slice_in_dim(kh, blk * MOBA_BLOCK, MOBA_BLOCK, axis=2)
        vown = lax.dynamic_slice_in_dim(vh, blk * MOBA_BLOCK, MOBA_BLOCK, axis=2)
        s_own = jnp.einsum("bhqd,bhkd->bhqk", qc, kown, preferred_element_type=jnp.float32) * scale
        kpos = blk * MOBA_BLOCK + jnp.arange(MOBA_BLOCK)
        s_own = jnp.where(kpos[None, :] <= qpos[:, None], s_own, NEG_INF)
        p = jax.nn.softmax(jnp.concatenate([s_own, s_sel], axis=-1), axis=-1).astype(vh.dtype)
        p_own = p[..., :MOBA_BLOCK]
        p_sel = p[..., MOBA_BLOCK:].reshape(B, H, MOBA_QCHUNK, topk, MOBA_BLOCK)
        return (jnp.einsum("bhqk,bhkd->bhqd", p_own, vown)
                + jnp.einsum("bhqnk,bhqnkd->bhqd", p_sel, vsel))

    out = lax.map(step, jnp.arange(sp // MOBA_QCHUNK))
    out = out.transpose(1, 0, 3, 2, 4).reshape(B, sp, H, dh)
    return out[:, :S]


def moba_mixer(xn, w_qkv, w_o, cos, sin):
    B, S, _ = xn.shape
    qkv = (xn @ w_qkv).reshape(B, S, 3, N_HEADS, HEAD_DIM)
    q = apply_rope(qkv[:, :, 0], cos, sin)
    k = apply_rope(qkv[:, :, 1], cos, sin)
    o = moba_attention(q, k, qkv[:, :, 2])
    return o.reshape(B, S, N_HEADS * HEAD_DIM) @ w_o


def causal_attention_blocked(q, k, v, scale):
    B, S, H, dk = q.shape
    dv = v.shape[-1]
    nq = S // ATTN_QBLOCK
    qb = q.reshape(B, nq, ATTN_QBLOCK, H, dk).transpose(1, 0, 2, 3, 4)
    key_pos = jnp.arange(S)

    def step(args):
        i, qi = args
        s = jnp.einsum("bqhd,bkhd->bhqk", qi, k, preferred_element_type=jnp.float32) * scale
        qpos = i * ATTN_QBLOCK + jnp.arange(ATTN_QBLOCK)
        s = jnp.where(key_pos[None, :] <= qpos[:, None], s, NEG_INF)
        p = jax.nn.softmax(s, axis=-1).astype(v.dtype)
        return jnp.einsum("bhqk,bkhd->bqhd", p, v)

    out = lax.map(step, (jnp.arange(nq), qb))
    return out.transpose(1, 0, 2, 3, 4).reshape(B, S, H, dv)


def mla_mixer(xn, w_in, q_norm, w_uq, kv_norm, w_ukv, w_o, cos_r, sin_r):
    B, S, _ = xn.shape
    c = xn @ w_in
    cq = rms_norm(c[..., :MLA_Q_LORA], q_norm)
    ckv = rms_norm(c[..., MLA_Q_LORA:MLA_Q_LORA + MLA_KV_LORA], kv_norm)
    kr = c[..., MLA_Q_LORA + MLA_KV_LORA:]
    qf = (cq @ w_uq).reshape(B, S, MLA_HEADS, MLA_NOPE + MLA_ROPE)
    q_rope = apply_rope(qf[..., MLA_NOPE:], cos_r, sin_r)
    kv = (ckv @ w_ukv).reshape(B, S, MLA_HEADS, MLA_NOPE + MLA_V)
    k_rope = jnp.broadcast_to(apply_rope(kr[:, :, None, :], cos_r, sin_r), (B, S, MLA_HEADS, MLA_ROPE))
    q = jnp.concatenate([qf[..., :MLA_NOPE], q_rope], axis=-1)
    k = jnp.concatenate([kv[..., :MLA_NOPE], k_rope], axis=-1)
    v = kv[..., MLA_NOPE:]
    o = causal_attention_blocked(q, k, v, (MLA_NOPE + MLA_ROPE) ** -0.5)
    return o.reshape(B, S, MLA_HEADS * MLA_V) @ w_o


def dilated_group_attention(q, k, v, window, dilation):
    B, S, H, dh = q.shape
    scale = dh ** -0.5
    span = window // dilation
    unit = DIL_BLOCK * dilation
    sp = -(-S // unit) * unit
    L = sp // dilation
    nb = L // DIL_BLOCK
    padw = ((0, 0), (0, sp - S), (0, 0), (0, 0))

    def to_blocks(t):
        t = jnp.pad(t, padw).reshape(B, L, dilation, H, dh).transpose(0, 3, 2, 1, 4)
        return t.reshape(B, H, dilation, nb, DIL_BLOCK, dh)

    def with_prev(t):
        prev = jnp.pad(t, ((0, 0), (0, 0), (0, 0), (1, 0), (0, 0), (0, 0)))[:, :, :, :-1]
        return jnp.concatenate([prev, t], axis=4)

    qb = to_blocks(q)
    kk = with_prev(to_blocks(k))
    vv = with_prev(to_blocks(v))
    s = jnp.einsum("bhrnqd,bhrnkd->bhrnqk", qb, kk, preferred_element_type=jnp.float32) * scale
    qi = jnp.arange(DIL_BLOCK)[:, None]
    kj = jnp.arange(2 * DIL_BLOCK)[None, :]
    dist = qi + DIL_BLOCK - kj
    key_sub = jnp.arange(nb)[:, None, None] * DIL_BLOCK - DIL_BLOCK + kj[None]
    mask = (dist >= 0)[None] & (dist <= span)[None] & (key_sub >= 0)
    s = jnp.where(mask, s, NEG_INF)
    lse = jax.nn.logsumexp(s, axis=-1)
    p = jnp.exp(s - lse[..., None]).astype(v.dtype)
    o = jnp.einsum("bhrnqk,bhrnkd->bhrnqd", p, vv)
    o = o.reshape(B, H, dilation, L, dh).transpose(0, 3, 2, 1, 4).reshape(B, sp, H, dh)[:, :S]
    lse = lse.reshape(B, H, dilation, L).transpose(0, 3, 2, 1).reshape(B, sp, H)[:, :S]
    return o, lse


def dilated_mixer(xn, w_qkv, w_o, cos, sin):
    B, S, _ = xn.shape
    G = len(DIL_GROUPS)
    qkv = (xn @ w_qkv).reshape(B, S, G, 3, DIL_HEADS, HEAD_DIM)
    outs, lses = [], []
    for g, (window, dilation) in enumerate(DIL_GROUPS):
        q = apply_rope(qkv[:, :, g, 0], cos, sin)
        k = apply_rope(qkv[:, :, g, 1], cos, sin)
        o, lse = dilated_group_attention(q, k, qkv[:, :, g, 2], window, dilation)
        outs.append(o)
        lses.append(lse)
    wts = jax.nn.softmax(jnp.stack(lses, axis=0), axis=0)
    o = jnp.einsum("gbsh,gbshd->bshd", wts, jnp.stack(outs, axis=0).astype(jnp.float32))
    return o.reshape(B, S, DIL_HEADS * HEAD_DIM).astype(xn.dtype) @ w_o


def stick_breaking_attention(q, k, v):
    B, S, H, dh = q.shape
    scale = dh ** -0.5
    nq = S // SB_QBLOCK
    qb = q.reshape(B, nq, SB_QBLOCK, H, dh).transpose(1, 0, 2, 3, 4)
    key_pos = jnp.arange(S)

    def step(args):
        i, qi = args
        z = jnp.einsum("bqhd,bkhd->bhqk", qi, k, preferred_element_type=jnp.float32) * scale
        qpos = i * SB_QBLOCK + jnp.arange(SB_QBLOCK)
        past = key_pos[None, :] < qpos[:, None]
        log_beta = jnp.where(past, jax.nn.log_sigmoid(z), NEG_INF)
        log_stay = jnp.where(past, jax.nn.log_sigmoid(-z), 0.0)
        after = jnp.concatenate([log_stay[..., 1:], jnp.zeros_like(log_stay[..., :1])], axis=-1)
        stay = lax.cumsum(after, axis=3, reverse=True)
        a = jnp.exp(log_beta + stay).astype(v.dtype)
        return jnp.einsum("bhqk,bkhd->bqhd", a, v)

    out = lax.map(step, (jnp.arange(nq), qb))
    return out.transpose(1, 0, 2, 3, 4).reshape(B, S, H, dh)


def stick_breaking_mixer(xn, w_qkv, w_o):
    B, S, _ = xn.shape
    qkv = (xn @ w_qkv).reshape(B, S, 3, SB_HEADS, HEAD_DIM)
    o = stick_breaking_attention(qkv[:, :, 0], qkv[:, :, 1], qkv[:, :, 2])
    return o.reshape(B, S, SB_HEADS * HEAD_DIM) @ w_o


def conv_ffn(xn, w_gate, conv_w, conv_b, w_up, w_down):
    S = xn.shape[1]
    g = xn @ w_gate
    gp = jnp.pad(g, ((0, 0), (CONV_WIDTH - 1, 0), (0, 0)))
    gc = conv_b + sum(conv_w[j] * gp[:, j:j + S] for j in range(CONV_WIDTH))
    h = jax.nn.gelu(gc, approximate=False) * (xn @ w_up)
    return h @ w_down


def _n_layers_of(kind):
    return len(range(kind, DEPTH, N_MIXERS))


def setup_inputs(seed: int = 0) -> dict:
    key = jax.random.key(seed)
    ks = jax.random.split(key, 24)
    f32 = jnp.float32
    nA, nB, nC, nD = (_n_layers_of(m) for m in range(N_MIXERS))
    hd = N_HEADS * HEAD_DIM

    def dense(k, shape, fan_in):
        return jax.random.normal(k, shape, f32) * (fan_in ** -0.5)

    def gain(k, shape):
        return 1.0 + 0.02 * jax.random.normal(k, shape, f32)

    return {
        "x": jax.random.normal(ks[0], (BATCH, SEQ, D_MODEL), f32),
        "norm_mix": gain(ks[1], (DEPTH, D_MODEL)),
        "norm_ffn": gain(ks[2], (DEPTH, D_MODEL)),
        "norm_final": gain(ks[3], (D_MODEL,)),
        "a_w_qkv": dense(ks[4], (nA, D_MODEL, 3 * hd), D_MODEL),
        "a_w_o": dense(ks[5], (nA, hd, D_MODEL), hd),
        "b_w_in": dense(ks[6], (nB, D_MODEL, MLA_Q_LORA + MLA_KV_LORA + MLA_ROPE), D_MODEL),
        "b_q_norm": gain(ks[7], (nB, MLA_Q_LORA)),
        "b_w_uq": dense(ks[8], (nB, MLA_Q_LORA, MLA_HEADS * (MLA_NOPE + MLA_ROPE)), MLA_Q_LORA),
        "b_kv_norm": gain(ks[9], (nB, MLA_KV_LORA)),
        "b_w_ukv": dense(ks[10], (nB, MLA_KV_LORA, MLA_HEADS * (MLA_NOPE + MLA_V)), MLA_KV_LORA),
        "b_w_o": dense(ks[11], (nB, MLA_HEADS * MLA_V, D_MODEL), MLA_HEADS * MLA_V),
        "c_w_qkv": dense(ks[12], (nC, D_MODEL, len(DIL_GROUPS) * 3 * DIL_HEADS * HEAD_DIM), D_MODEL),
        "c_w_o": dense(ks[13], (nC, DIL_HEADS * HEAD_DIM, D_MODEL), DIL_HEADS * HEAD_DIM),
        "d_w_qkv": dense(ks[14], (nD, D_MODEL, 3 * SB_HEADS * HEAD_DIM), D_MODEL),
        "d_w_o": dense(ks[15], (nD, SB_HEADS * HEAD_DIM, D_MODEL), SB_HEADS * HEAD_DIM),
        "ffn_w_gate": dense(ks[16], (DEPTH, D_MODEL, D_FF), D_MODEL),
        "ffn_conv_w": dense(ks[17], (DEPTH, CONV_WIDTH, D_FF), CONV_WIDTH),
        "ffn_conv_b": 0.02 * jax.random.normal(ks[18], (DEPTH, D_FF), f32),
        "ffn_w_up": dense(ks[19], (DEPTH, D_MODEL, D_FF), D_MODEL),
        "ffn_w_down": dense(ks[20], (DEPTH, D_FF, D_MODEL), D_FF),
    }


def reference(x, norm_mix, norm_ffn, norm_final,
              a_w_qkv, a_w_o,
              b_w_in, b_q_norm, b_w_uq, b_kv_norm, b_w_ukv, b_w_o,
              c_w_qkv, c_w_o,
              d_w_qkv, d_w_o,
              ffn_w_gate, ffn_conv_w, ffn_conv_b, ffn_w_up, ffn_w_down):
    S = x.shape[1]
    pos = jnp.arange(S, dtype=jnp.float32)
    cos_h, sin_h = rope_tables(pos, HEAD_DIM)
    cos_r, sin_r = rope_tables(pos, MLA_ROPE)
    h = x
    for i in range(DEPTH):
        kind, j = i % N_MIXERS, i // N_MIXERS
        hn = rms_norm(h, norm_mix[i])
        if kind == 0:
            mix = moba_mixer(hn, a_w_qkv[j], a_w_o[j], cos_h, sin_h)
        elif kind == 1:
            mix = mla_mixer(hn, b_w_in[j], b_q_norm[j], b_w_uq[j], b_kv_norm[j], b_w_ukv[j], b_w_o[j], cos_r, sin_r)
        elif kind == 2:
            mix = dilated_mixer(hn, c_w_qkv[j], c_w_o[j], cos_h, sin_h)
        else:
            mix = stick_breaking_mixer(hn, d_w_qkv[j], d_w_o[j])
        h = h + mix.astype(h.dtype)
        f = conv_ffn(rms_norm(h, norm_ffn[i]), ffn_w_gate[i], ffn_conv_w[i], ffn_conv_b[i], ffn_w_up[i], ffn_w_down[i])
        h = h + f.astype(h.dtype)
    return rms_norm(h, norm_final)
```

```python
import functools
import math

import jax
import jax.numpy as jnp
from jax import lax
from jax.experimental import pallas as pl
from jax.experimental.pallas import tpu as pltpu

D_MODEL = 1024
HEAD_DIM = 64
N_HEADS = 16
ROPE_THETA = 10000.0
RMS_EPS = 1e-6
NEG_INF = -1e30
MOBA_BLOCK = 256
MOBA_TOPK = 3
MLA_Q_LORA = 384
MLA_KV_LORA = 256
MLA_NOPE = 64
MLA_ROPE = 32
MLA_V = 64
MLA_QK_PAD = 128
DIL_GROUPS = ((128, 1), (512, 4), (2048, 16))
DIL_BLOCK = 128
D_FF = 2816
CONV_WIDTH = 3
LANES = 128
SB_UNDERFLOW = -104.0
VMEM_LIMIT = 56 * 1024 * 1024

F32 = jnp.float32
BF16 = jnp.bfloat16


def _cparams(sem):
    return pltpu.CompilerParams(dimension_semantics=sem, vmem_limit_bytes=VMEM_LIMIT)


def _dot(a, b):
    return jnp.dot(a, b, preferred_element_type=F32)


def _dot_nt(a, b):
    return lax.dot_general(a, b, (((1,), (1,)), ((), ())), preferred_element_type=F32)


def _rms(x, g):
    return x * lax.rsqrt(jnp.mean(x * x, axis=-1, keepdims=True) + RMS_EPS) * g


def _norm_matmul_kernel(*refs, rope, half, tn):
    if rope:
        x_ref, g_ref, w_ref, c_ref, sp_ref, sm_ref, o_ref = refs
    else:
        x_ref, g_ref, w_ref, o_ref = refs
    xn = _rms(x_ref[...], g_ref[...]).astype(BF16)
    y = _dot(xn, w_ref[...])
    if rope:
        reps = tn // LANES
        c = jnp.tile(c_ref[...], (1, reps))
        sp = jnp.tile(sp_ref[...], (1, reps))
        sm = jnp.tile(sm_ref[...], (1, reps))
        y = y * c + pltpu.roll(y, half, axis=1) * sp + pltpu.roll(y, tn - half, axis=1) * sm
    o_ref[...] = y.astype(o_ref.dtype)


def _norm_matmul(x, g, w, seq, *, rope=None, half=0, out_dtype=BF16, tm=512, tn=None):
    m, d = x.shape
    n = w.shape[1]
    tn = n if tn is None else tn
    grid = (n // tn, m // tm)
    in_specs = [pl.BlockSpec((tm, d), lambda j, i: (i, 0)),
                pl.BlockSpec((1, d), lambda j, i: (0, 0)),
                pl.BlockSpec((d, tn), lambda j, i: (0, j))]
    args = [x, g.reshape(1, d), w]
    if rope is not None:
        nblk = seq // tm
        for t in rope:
            in_specs.append(pl.BlockSpec((tm, LANES), lambda j, i: (i % nblk, 0)))
            args.append(t)
    return pl.pallas_call(
        functools.partial(_norm_matmul_kernel, rope=rope is not None, half=half, tn=tn),
        grid=grid, in_specs=in_specs,
        out_specs=pl.BlockSpec((tm, tn), lambda j, i: (i, j)),
        out_shape=jax.ShapeDtypeStruct((m, n), out_dtype),
        compiler_params=_cparams(("parallel", "parallel")),
        name="norm_matmul",
    )(*args)


def _mla_proj_kernel(x_ref, g_ref, win_ref, qn_ref, kvn_ref, wuq_ref, wuk_ref, wuv_ref,
                     c_ref, sp_ref, sm_ref, q_ref, k_ref, v_ref):
    half = MLA_ROPE // 2
    xn = _rms(x_ref[...], g_ref[...]).astype(BF16)
    c = _dot(xn, win_ref[...])
    cq = _rms(c[:, :MLA_Q_LORA], qn_ref[...]).astype(BF16)
    ckv = _rms(c[:, MLA_Q_LORA:MLA_Q_LORA + MLA_KV_LORA], kvn_ref[...]).astype(BF16)
    kr = c[:, MLA_Q_LORA + MLA_KV_LORA:]

    def rot(y, width):
        reps = width // LANES
        cc = jnp.tile(c_ref[...], (1, reps))
        sp = jnp.tile(sp_ref[...], (1, reps))
        sm = jnp.tile(sm_ref[...], (1, reps))
        return y * cc + pltpu.roll(y, half, axis=1) * sp + pltpu.roll(y, width - half, axis=1) * sm

    nq = wuq_ref.shape[1]
    q = rot(_dot(cq, wuq_ref[...]), nq)
    q_ref[...] = q.astype(q_ref.dtype)
    kr = rot(kr, LANES)
    k =_dot(ckv, wuk_ref[...]) + jnp.tile(kr, (1, nq // LANES))
    k_ref[...] = k.astype(k_ref.dtype)
    v_ref[...] = _dot(ckv, wuv_ref[...]).astype(v_ref.dtype)


def _mla_proj(x, g, win, qn, kvn, wuq, wuk, wuv, rope, seq, tm=512):
    m, d = x.shape
    nblk = seq // tm
    full = lambda a: pl.BlockSpec(a.shape, lambda i: (0, 0))
    qn2, kvn2, g2 = qn.reshape(1, -1), kvn.reshape(1, -1), g.reshape(1, -1)
    in_specs = [pl.BlockSpec((tm, d), lambda i: (i, 0)), full(g2), full(win), full(qn2), full(kvn2),
                full(wuq), full(wuk), full(wuv)]
    in_specs += [pl.BlockSpec((tm, LANES), lambda i: (i % nblk, 0))] * 3
    nq, nv = wuq.shape[1], wuv.shape[1]
    return pl.pallas_call(
        _mla_proj_kernel, grid=(m // tm,), in_specs=in_specs,
        out_specs=[pl.BlockSpec((tm, nq), lambda i: (i, 0)), pl.BlockSpec((tm, nq), lambda i: (i, 0)),
                   pl.BlockSpec((tm, nv), lambda i: (i, 0))],
        out_shape=[jax.ShapeDtypeStruct((m, nq), BF16), jax.ShapeDtypeStruct((m, nq), BF16),
                   jax.ShapeDtypeStruct((m, nv), BF16)],
        compiler_params=_cparams(("parallel",)),
        name="mla_proj",
    )(x, g2, win, qn2, kvn2, wuq, wuk, wuv, *rope)


def _out_proj_kernel(a_ref, w_ref, h_ref, o_ref):
    o_ref[...] = h_ref[...] + _dot(a_ref[...], w_ref[...])


def _out_proj(a, w, h, tm=512):
    m, k = a.shape
    d = w.shape[1]
    return pl.pallas_call(
        _out_proj_kernel, grid=(m // tm,),
        in_specs=[pl.BlockSpec((tm, k), lambda i: (i, 0)), pl.BlockSpec((k, d), lambda i: (0, 0)),
                  pl.BlockSpec((tm, d), lambda i: (i, 0))],
        out_specs=pl.BlockSpec((tm, d), lambda i: (i, 0)),
        out_shape=jax.ShapeDtypeStruct((m, d), F32),
        compiler_params=_cparams(("parallel",)),
        name="out_proj",
    )(a, w, h)


def _dil_out_proj_kernel(a0_ref, a1_ref, a2_ref, w_ref, h_ref, o_ref):
    xs = [a0_ref[...], a1_ref[...], a2_ref[...]]
    width = xs[0].shape[1]
    lane = lax.broadcasted_iota(jnp.int32, xs[0].shape, 1) % LANES
    is_out = lane < HEAD_DIM
    mx = jnp.maximum(jnp.maximum(xs[0], xs[1]), xs[2])
    ws = [jnp.exp(x - mx) for x in xs]
    den = ws[0] + ws[1] + ws[2]
    comb = jnp.zeros_like(xs[0])
    for x, wgt in zip(xs, ws):
        wn = pltpu.roll(wgt / den, width - HEAD_DIM, axis=1)
        comb = comb + wn * x
    comb = jnp.where(is_out, comb, 0.0).astype(BF16)
    o_ref[...] = h_ref[...] + _dot(comb, w_ref[...])


def _dil_out_proj(a_list, w_exp, h, tm=256):
    m, k = a_list[0].shape
    d = w_exp.shape[1]
    a_spec = pl.BlockSpec((tm, k), lambda i: (i, 0))
    return pl.pallas_call(
        _dil_out_proj_kernel, grid=(m // tm,),
        in_specs=[a_spec, a_spec, a_spec, pl.BlockSpec((k, d), lambda i: (0, 0)),
                  pl.BlockSpec((tm, d), lambda i: (i, 0))],
        out_specs=pl.BlockSpec((tm, d), lambda i: (i, 0)),
        out_shape=jax.ShapeDtypeStruct((m, d), F32),
        compiler_params=_cparams(("parallel",)),
        name="dil_out_proj",
    )(*a_list, w_exp, h)


def _gelu_exact(x):
    return 0.5 * x * (1.0 + lax.erf(x * (1.0 / math.sqrt(2.0))))


def _ffn_kernel(x_ref, g_ref, wg_ref, wu_ref, wd_ref, cw_ref, cb_ref, o_ref, xn_ref, acc_ref, carry_ref,
                *, nblk, fc, final_norm):
    i, c = pl.program_id(0), pl.program_id(1)
    nc = pl.num_programs(1)
    tm = x_ref.shape[0]

    @pl.when(c == 0)
    def _():
        xn_ref[...] = _rms(x_ref[...], g_ref[...]).astype(BF16)
        acc_ref[...] = jnp.zeros_like(acc_ref)

    @pl.when(jnp.logical_and(c == 0, i % nblk == 0))
    def _():
        carry_ref[...] = jnp.zeros_like(carry_ref)

    xn = xn_ref[...]
    gate = _dot(xn, wg_ref[...])
    up = _dot(xn, wu_ref[...])
    col = pl.multiple_of(c * fc, LANES)
    prev = carry_ref[:, pl.ds(col, fc)]
    row = lax.broadcasted_iota(jnp.int32, gate.shape, 0)
    p1, p2 = prev[7:8, :], prev[6:7, :]
    s1 = jnp.where(row == 0, p1, pltpu.roll(gate, 1, axis=0))
    s2 = jnp.where(row == 0, p2, jnp.where(row == 1, p1, pltpu.roll(gate, 2, axis=0)))
    carry_ref[:, pl.ds(col, fc)] = gate[tm - 8:, :]
    cw = cw_ref[0]
    gc = cb_ref[0] + cw[0:1, :] * s2 + cw[1:2, :] * s1 + cw[2:3, :] * gate
    hidden = (_gelu_exact(gc) * up).astype(BF16)
    acc_ref[...] += _dot(hidden, wd_ref[...])

    @pl.when(c == nc - 1)
    def _():
        o_ref[...] = x_ref[...] + acc_ref[...]


def _ffn(h, g, wg, wu, wd, cw, cb, seq, tm=512, fc=1408):
    m, d = h.shape
    f = wg.shape[1]
    nc = f // fc
    cw3 = cw.reshape(CONV_WIDTH, nc, fc).transpose(1, 0, 2)
    cb3 = cb.reshape(nc, 1, fc)
    return pl.pallas_call(
        functools.partial(_ffn_kernel, nblk=seq // tm, fc=fc, final_norm=False),
        grid=(m // tm, nc),
        in_specs=[pl.BlockSpec((tm, d), lambda i, c: (i, 0)),
                  pl.BlockSpec((1, d), lambda i, c: (0, 0)),
                  pl.BlockSpec((d, fc), lambda i, c: (0, c)),
                  pl.BlockSpec((d, fc), lambda i, c: (0, c)),
                  pl.BlockSpec((fc, d), lambda i, c: (c, 0)),
                  pl.BlockSpec((1, CONV_WIDTH, fc), lambda i, c: (c, 0, 0)),
                  pl.BlockSpec((1, 1, fc), lambda i, c: (c, 0, 0))],
        out_specs=pl.BlockSpec((tm, d), lambda i, c: (i, 0)),
        out_shape=jax.ShapeDtypeStruct((m, d), F32),
        scratch_shapes=[pltpu.VMEM((tm, d), BF16), pltpu.VMEM((tm, d), F32), pltpu.VMEM((8, f), F32)],
        compiler_params=_cparams(("arbitrary", "arbitrary")),
        name="conv_ffn",
    )(h, g.reshape(1, d), wg, wu, wd, cw3, cb3)


def _final_norm_kernel(x_ref, g_ref, o_ref):
    o_ref[...] = _rms(x_ref[...], g_ref[...])


def _final_norm(h, g, tm=1024):
    m, d = h.shape
    return pl.pallas_call(
        _final_norm_kernel, grid=(m // tm,),
        in_specs=[pl.BlockSpec((tm, d), lambda i: (i, 0)), pl.BlockSpec((1, d), lambda i: (0, 0))],
        out_specs=pl.BlockSpec((tm, d), lambda i: (i, 0)),
        out_shape=jax.ShapeDtypeStruct((m, d), F32),
        compiler_params=_cparams(("parallel",)),
        name="final_norm",
    )(h, g.reshape(1, d))


def _softmax_step(s, v, m, l, acc):
    m_new = jnp.maximum(m, jnp.max(s, axis=1, keepdims=True))
    alpha = jnp.exp(m - m_new)
    p = jnp.exp(s - m_new)
    l = alpha * l + jnp.sum(p, axis=1, keepdims=True)
    acc = alpha * acc + _dot(p.astype(BF16), v)
    return m_new, l, acc


def _causal_mask(i, j, tq, tk):
    row = i * tq + lax.broadcasted_iota(jnp.int32, (tq, tk), 0)
    col = j * tk + lax.broadcasted_iota(jnp.int32, (tq, tk), 1)
    return col <= row


def _mla_attn_kernel(q_ref, k_ref, v_ref, o_ref, *, tq, tk, scale):
    i = pl.program_id(2)
    q = q_ref[0, 0]
    dv = v_ref.shape[-1]

    def tile(j, carry, masked):
        start = pl.multiple_of(j * tk, tk)
        k = k_ref[0, 0, pl.ds(start, tk), :]
        v = v_ref[0, 0, pl.ds(start, tk), :]
        s = _dot_nt(q, k) * scale
        if masked:
            s = jnp.where(_causal_mask(i, j, tq, tk), s, NEG_INF)
        return _softmax_step(s, v, *carry)

    init = (jnp.full((tq, 1), NEG_INF, F32), jnp.zeros((tq, 1), F32), jnp.zeros((tq, dv), F32))
    n_full = i * (tq // tk)
    carry = lax.fori_loop(0, n_full, lambda j, c: tile(j, c, False), init)
    for jj in range(tq // tk):
        carry = tile(n_full + jj, carry, True)
    _, l, acc = carry
    o_ref[0, 0] = (acc / l).astype(o_ref.dtype)


def _mla_attention(q, k, v, scale, tq=256, tk=256):
    b, h, s, dk = q.shape
    dv = v.shape[-1]
    return pl.pallas_call(
        functools.partial(_mla_attn_kernel, tq=tq, tk=tk, scale=scale),
        grid=(b, h, s // tq),
        in_specs=[pl.BlockSpec((1, 1, tq, dk), lambda b_, h_, i: (b_, h_, i, 0)),
                  pl.BlockSpec((1, 1, s, dk), lambda b_, h_, i: (b_, h_, 0, 0)),
                  pl.BlockSpec((1, 1, s, dv), lambda b_, h_, i: (b_, h_, 0, 0))],
        out_specs=pl.BlockSpec((1, 1, tq, dv), lambda b_, h_, i: (b_, h_, i, 0)),
        out_shape=jax.ShapeDtypeStruct((b, h, s, dv), BF16),
        compiler_params=_cparams(("parallel", "parallel", "arbitrary")),
        name="mla_attention",
    )(q, k, v)


def _moba_kernel(q_ref, k_ref, v_ref, o_ref, kmean_ref, *, nb):
    i = pl.program_id(2)
    blk = MOBA_BLOCK
    dh = q_ref.shape[-1]

    @pl.when(i == 0)
    def _():
        kmean_ref[...] = jnp.zeros_like(kmean_ref)
        for n in range(nb):
            rows = k_ref[0, 0, n * blk:(n + 1) * blk, :].astype(F32)
            kmean_ref[n:n + 1, :] = jnp.sum(rows, axis=0, keepdims=True) * (1.0 / blk)

    q = q_ref[0, 0]
    km = kmean_ref[...]
    km_hi = km.astype(BF16)
    km_lo = (km - km_hi.astype(F32)).astype(BF16)
    gate = _dot_nt(q, km_hi) + _dot_nt(q, km_lo)
    lane = lax.broadcasted_iota(jnp.int32, gate.shape, 1)
    lane_f = lane.astype(F32)
    past = lane < i
    g = jnp.where(past, gate, NEG_INF)
    picked = jnp.zeros(gate.shape, F32)
    for _ in range(MOBA_TOPK):
        mx = jnp.max(g, axis=1, keepdims=True)
        first = jnp.min(jnp.where(g == mx, lane_f, float(LANES)), axis=1, keepdims=True)
        hit = lane_f == first
        picked = jnp.where(hit, 1.0, picked)
        g = jnp.where(hit, -jnp.inf, g)
    sel = jnp.where(past, picked, 0.0)

    qs = q * 0.125
    own = pl.multiple_of(i * blk, blk)
    s = _dot_nt(qs, k_ref[0, 0, pl.ds(own, blk), :])
    s = jnp.where(_causal_mask(0, 0, blk, blk), s, NEG_INF)
    init = _softmax_step(s, v_ref[0, 0, pl.ds(own, blk), :],
                         jnp.full((blk, 1), NEG_INF, F32), jnp.zeros((blk, 1), F32), jnp.zeros((blk, dh), F32))

    def tile(j, carry):
        start = pl.multiple_of(j * blk, blk)
        chosen = jnp.sum(jnp.where(lane == j, sel, 0.0), axis=1, keepdims=True) > 0.5
        s = _dot_nt(qs, k_ref[0, 0, pl.ds(start, blk), :])
        s = jnp.where(chosen, s, NEG_INF)
        return _softmax_step(s, v_ref[0, 0, pl.ds(start, blk), :], *carry)

    _, l, acc = lax.fori_loop(0, i, tile, init)
    o_ref[0, 0] = (acc / l).astype(o_ref.dtype)


def _moba_attention(q, k, v):
    b, h, s, dh = q.shape
    nb = s // MOBA_BLOCK
    return pl.pallas_call(
        functools.partial(_moba_kernel, nb=nb),
        grid=(b, h, nb),
        in_specs=[pl.BlockSpec((1, 1, MOBA_BLOCK, dh), lambda b_, h_, i: (b_, h_, i, 0)),
                  pl.BlockSpec((1, 1, s, dh), lambda b_, h_, i: (b_, h_, 0, 0)),
                  pl.BlockSpec((1, 1, s, dh), lambda b_, h_, i: (b_, h_, 0, 0))],
        out_specs=pl.BlockSpec((1, 1, MOBA_BLOCK, dh), lambda b_, h_, i: (b_, h_, i, 0)),
        out_shape=jax.ShapeDtypeStruct((b, h, s, dh), BF16),
        scratch_shapes=[pltpu.VMEM((LANES, dh), F32)],
        compiler_params=_cparams(("parallel", "parallel", "arbitrary")),
        name="moba_attention",
    )(q, k, v)


def _sb_kernel(q_ref, k_ref, v_ref, o_ref, acc_ref, stay_ref, *, t):
    i = pl.program_id(2)
    q = q_ref[0, 0] * 0.125
    r = lax.broadcasted_iota(jnp.int32, (t, t), 0)
    c = lax.broadcasted_iota(jnp.int32, (t, t), 1)
    later = jnp.where(r > c, 1.0, 0.0).astype(BF16)

    def tile(j, diag):
        start = pl.multiple_of(j * t, t)
        z = _dot_nt(q, k_ref[0, 0, pl.ds(start, t), :])
        lp = jnp.log1p(jnp.exp(-jnp.abs(z)))
        log_beta = jnp.minimum(z, 0.0) - lp
        log_stay = log_beta - z
        if diag:
            past = c < r
            log_stay = jnp.where(past, log_stay, 0.0)
        hi = log_stay.astype(BF16)
        r1 = log_stay - hi.astype(F32)
        mid = r1.astype(BF16)
        lo = (r1 - mid.astype(F32)).astype(BF16)
        inner = _dot(hi, later) + _dot(mid, later) + _dot(lo, later)
        a = jnp.exp(log_beta + inner + stay_ref[...])
        if diag:
            a = jnp.where(past, a, 0.0)
        acc_ref[...] += _dot(a.astype(BF16), v_ref[0, 0, pl.ds(start, t), :])
        stay_ref[...] += jnp.sum(log_stay, axis=1, keepdims=True)
        return jnp.max(stay_ref[...])

    acc_ref[...] = jnp.zeros_like(acc_ref)
    stay_ref[...] = jnp.zeros_like(stay_ref)
    top = tile(i, True)

    def cond(state):
        j, top = state
        return jnp.logical_and(j >= 0, top > SB_UNDERFLOW)

    def body(state):
        j, _ = state
        return j - 1, tile(j, False)

    lax.while_loop(cond, body, (i - 1, top))
    o_ref[0, 0] = acc_ref[...].astype(o_ref.dtype)


def _sb_attention(q, k, v, t=256):
    b, h, s, dh = q.shape
    return pl.pallas_call(
        functools.partial(_sb_kernel, t=t),
        grid=(b, h, s // t),
        in_specs=[pl.BlockSpec((1, 1, t, dh), lambda b_, h_, i: (b_, h_, i, 0)),
                  pl.BlockSpec((1, 1, s, dh), lambda b_, h_, i: (b_, h_, 0, 0)),
                  pl.BlockSpec((1, 1, s, dh), lambda b_, h_, i: (b_, h_, 0, 0))],
        out_specs=pl.BlockSpec((1, 1, t, dh), lambda b_, h_, i: (b_, h_, i, 0)),
        out_shape=jax.ShapeDtypeStruct((b, h, s, dh), BF16),
        scratch_shapes=[pltpu.VMEM((t, dh), F32), pltpu.VMEM((t, 1), F32)],
        compiler_params=_cparams(("parallel", "parallel", "arbitrary")),
        name="sb_attention",
    )(q, k, v)


def _dil_kernel(q_ref, k_ref, v_ref, o_ref, *, nblk):
    t = DIL_BLOCK
    dh = q_ref.shape[-1]
    r = lax.broadcasted_iota(jnp.int32, (t, t), 0)
    c = lax.broadcasted_iota(jnp.int32, (t, t), 1)

    def block(n, first):
        start = pl.multiple_of(n * t, t)
        q = q_ref[0, pl.ds(start, t), :] * 0.125
        s_cur = jnp.where(c <= r, _dot_nt(q, k_ref[0, pl.ds(start, t), :]), NEG_INF)
        m = jnp.max(s_cur, axis=1, keepdims=True)
        if not first:
            prev = pl.multiple_of(start - t, t)
            s_prev = jnp.where(c >= r, _dot_nt(q, k_ref[0, pl.ds(prev, t), :]), NEG_INF)
            m = jnp.maximum(m, jnp.max(s_prev, axis=1, keepdims=True))
        p_cur = jnp.exp(s_cur - m)
        l = jnp.sum(p_cur, axis=1, keepdims=True)
        acc = _dot(p_cur.astype(BF16), v_ref[0, pl.ds(start, t), :])
        if not first:
            p_prev = jnp.exp(s_prev - m)
            l = l + jnp.sum(p_prev, axis=1, keepdims=True)
            acc = acc + _dot(p_prev.astype(BF16), v_ref[0, pl.ds(prev, t), :])
        lse = m + jnp.log(l)
        o_ref[0, pl.ds(start, t), :] = jnp.concatenate([acc / l, jnp.broadcast_to(lse, (t, dh))], axis=1)

    block(0, True)

    def body(n, _):
        block(n, False)
        return 0

    lax.fori_loop(1, nblk, body, 0)


def _dil_attention(q, k, v):
    nseq, length, dh = q.shape
    spec = pl.BlockSpec((1, length, dh), lambda n: (n, 0, 0))
    return pl.pallas_call(
        functools.partial(_dil_kernel, nblk=length // DIL_BLOCK),
        grid=(nseq,),
        in_specs=[spec, spec, spec],
        out_specs=pl.BlockSpec((1, length, 2 * dh), lambda n: (n, 0, 0)),
        out_shape=jax.ShapeDtypeStruct((nseq, length, 2 * dh), F32),
        compiler_params=_cparams(("parallel",)),
        name="dil_attention",
    )(q, k, v)


def _rope_tables(seq, dim, first_lane, period):
    half = dim // 2
    inv = ROPE_THETA ** (-jnp.arange(0, dim, 2, dtype=F32) / dim)
    ang = jnp.arange(seq, dtype=F32)[:, None] * inv[None, :]
    cos, sin = jnp.cos(ang), jnp.sin(ang)
    zeros = jnp.zeros_like(sin)
    pad_lo = jnp.zeros((seq, first_lane), F32)
    pad_hi = jnp.zeros((seq, period - first_lane - dim), F32)
    c = jnp.concatenate([pad_lo + 1.0, cos, cos, pad_hi + 1.0], axis=1)
    s_plus = jnp.concatenate([pad_lo, zeros, sin, pad_hi], axis=1)
    s_minus = jnp.concatenate([pad_lo, -sin, zeros, pad_hi], axis=1)
    reps = LANES // period
    return tuple(jnp.tile(t, (1, reps)) for t in (c, s_plus, s_minus))


def _heads(y, b, s, nh, dh):
    return y.reshape(b, s, nh, dh).transpose(0, 2, 1, 3)


def _unheads(o):
    b, h, s, dh = o.shape
    return o.transpose(0, 2, 1, 3).reshape(b * s, h * dh)


def _mla_weights(w_in, w_uq, w_ukv):
    d = w_in.shape[0]
    ql, kvl = MLA_Q_LORA, MLA_KV_LORA
    w_kr = jnp.zeros((d, LANES), F32).at[:, MLA_NOPE:MLA_NOPE + MLA_ROPE].set(w_in[:, ql + kvl:])
    win = jnp.concatenate([w_in[:, :ql + kvl], w_kr], axis=1)
    wuq = w_uq.reshape(ql, N_HEADS, MLA_NOPE + MLA_ROPE)
    wuq = jnp.pad(wuq, ((0, 0), (0, 0), (0, MLA_QK_PAD - MLA_NOPE - MLA_ROPE))).reshape(ql, N_HEADS * MLA_QK_PAD)
    wukv = w_ukv.reshape(kvl, N_HEADS, MLA_NOPE + MLA_V)
    wuk = jnp.pad(wukv[:, :, :MLA_NOPE], ((0, 0), (0, 0), (0, MLA_QK_PAD - MLA_NOPE))).reshape(kvl, N_HEADS * MLA_QK_PAD)
    wuv = wukv[:, :, MLA_NOPE:].reshape(kvl, N_HEADS * MLA_V)
    return win.astype(BF16), wuq.astype(BF16), wuk.astype(BF16), wuv.astype(BF16)


def kernel(x, norm_mix, norm_ffn, norm_final, a_w_qkv, a_w_o, b_w_in, b_q_norm, b_w_uq, b_kv_norm, b_w_ukv, b_w_o,
           c_w_qkv, c_w_o, d_w_qkv, d_w_o, ffn_w_gate, ffn_conv_w, ffn_conv_b, ffn_w_up, ffn_w_down):
    b, s, d = x.shape
    m = b * s
    hd = N_HEADS * HEAD_DIM
    depth = norm_mix.shape[0]
    rope_h = _rope_tables(s, HEAD_DIM, 0, HEAD_DIM)
    rope_r = _rope_tables(s, MLA_ROPE, MLA_NOPE, MLA_QK_PAD)
    h = x.reshape(m, d)
    for i in range(depth):
        kind, j = i % 4, i // 4
        g = norm_mix[i]
        if kind == 0:
            w = a_w_qkv[j].astype(BF16)
            qk = _norm_matmul(h, g, w[:, :2 * hd], s, rope=rope_h, half=HEAD_DIM // 2)
            v = _norm_matmul(h, g, w[:, 2 * hd:], s)
            q = _heads(qk[:, :hd], b, s, N_HEADS, HEAD_DIM)
            k = _heads(qk[:, hd:], b, s, N_HEADS, HEAD_DIM)
            o = _moba_attention(q, k, _heads(v, b, s, N_HEADS, HEAD_DIM))
            h = _out_proj(_unheads(o), a_w_o[j].astype(BF16), h)
        elif kind == 1:
            win, wuq, wuk, wuv = _mla_weights(b_w_in[j], b_w_uq[j], b_w_ukv[j])
            q, k, v = _mla_proj(h, g, win, b_q_norm[j], b_kv_norm[j], wuq, wuk, wuv, rope_r, s)
            o = _mla_attention(_heads(q, b, s, N_HEADS, MLA_QK_PAD), _heads(k, b, s, N_HEADS, MLA_QK_PAD),
                               _heads(v, b, s, N_HEADS, MLA_V), (MLA_NOPE + MLA_ROPE) ** -0.5)
            h = _out_proj(_unheads(o), b_w_o[j].astype(BF16), h)
        elif kind == 2:
            ng = len(DIL_GROUPS)
            w = c_w_qkv[j].astype(BF16).reshape(d, ng, 3, hd)
            qk = _norm_matmul(h, g, w[:, :, :2].reshape(d, ng * 2 * hd), s, rope=rope_h, half=HEAD_DIM // 2,
                              tn=2 * hd)
            v = _norm_matmul(h, g, w[:, :, 2].reshape(d, ng * hd), s)
            qk = qk.reshape(b, s, ng, 2, N_HEADS, HEAD_DIM)
            v = v.reshape(b, s, ng, N_HEADS, HEAD_DIM)
            outs = []
            for gi, (window, dil) in enumerate(DIL_GROUPS):
                length = s // dil

                def dilate(t):
                    return t.reshape(b, length, dil, N_HEADS, HEAD_DIM).transpose(0, 3, 2, 1, 4).reshape(
                        b * N_HEADS * dil, length, HEAD_DIM)

                og = _dil_attention(dilate(qk[:, :, gi, 0]), dilate(qk[:, :, gi, 1]), dilate(v[:, :, gi]))
                og = og.reshape(b, N_HEADS, dil, length, 2 * HEAD_DIM).transpose(0, 3, 2, 1, 4)
                outs.append(og.reshape(m, N_HEADS * 2 * HEAD_DIM))
            w_o = c_w_o[j].astype(BF16).reshape(N_HEADS, HEAD_DIM, d)
            w_o = jnp.pad(w_o, ((0, 0), (0, HEAD_DIM), (0, 0))).reshape(N_HEADS * 2 * HEAD_DIM, d)
            h = _dil_out_proj(outs, w_o, h)
        else:
            w = d_w_qkv[j].astype(BF16)
            qkv = _norm_matmul(h, g, w, s).reshape(b, s, 3, N_HEADS, HEAD_DIM)
            o = _sb_attention(qkv[:, :, 0].transpose(0, 2, 1, 3), qkv[:, :, 1].transpose(0, 2, 1, 3),
                              qkv[:, :, 2].transpose(0, 2, 1, 3))
            h = _out_proj(_unheads(o), d_w_o[j].astype(BF16), h)
        h = _ffn(h, norm_ffn[i], ffn_w_gate[i].astype(BF16), ffn_w_up[i].astype(BF16), ffn_w_down[i].astype(BF16),
                 ffn_conv_w[i], ffn_conv_b[i], s)
    return _final_norm(h, norm_final).reshape(b, s, d)
```

```python
import functools
import math

import jax
import jax.numpy as jnp
from jax import lax
from jax.experimental import pallas as pl
from jax.experimental.pallas import tpu as pltpu

D_MODEL = 1024
HEAD_DIM = 64
N_HEADS = 16
ROPE_THETA = 10000.0
RMS_EPS = 1e-6
NEG_INF = -1e30
MOBA_BLOCK = 256
MOBA_TOPK = 3
MLA_Q_LORA = 384
MLA_KV_LORA = 256
MLA_NOPE = 64
MLA_ROPE = 32
MLA_V = 64
MLA_QK_PAD = 128
DIL_GROUPS = ((128, 1), (512, 4), (2048, 16))
DIL_BLOCK = 128
D_FF = 2816
CONV_WIDTH = 3
LANES = 128
SB_UNDERFLOW = -104.0
VMEM_LIMIT = 56 * 1024 * 1024
LOG2E = 1.4426950408889634
FLASH_TQ = 512
FLASH_TK = 256
ONES_ROWS = 16
DIL_ILP = 4

F32 = jnp.float32
BF16 = jnp.bfloat16


def _cparams(sem):
    return pltpu.CompilerParams(dimension_semantics=sem, vmem_limit_bytes=VMEM_LIMIT)


def _dot(a, b):
    return jnp.dot(a, b, preferred_element_type=F32)


def _dot_nt(a, b):
    return lax.dot_general(a, b, (((1,), (1,)), ((), ())), preferred_element_type=F32)


def _rms(x, g):
    return x * lax.rsqrt(jnp.mean(x * x, axis=-1, keepdims=True) + RMS_EPS) * g


def _norm_matmul_kernel(*refs, rope, half, tn):
    if rope:
        x_ref, g_ref, w_ref, c_ref, sp_ref, sm_ref, o_ref = refs
    else:
        x_ref, g_ref, w_ref, o_ref = refs
    xn = _rms(x_ref[...], g_ref[...]).astype(BF16)
    y = _dot(xn, w_ref[...])
    if rope:
        reps = tn // LANES
        c = jnp.tile(c_ref[...], (1, reps))
        sp = jnp.tile(sp_ref[...], (1, reps))
        sm = jnp.tile(sm_ref[...], (1, reps))
        y = y * c + pltpu.roll(y, half, axis=1) * sp + pltpu.roll(y, tn - half, axis=1) * sm
    o_ref[...] = y.astype(o_ref.dtype)


def _norm_matmul(x, g, w, seq, *, rope=None, half=0, out_dtype=BF16, tm=512, tn=None):
    m, d = x.shape
    n = w.shape[1]
    tn = n if tn is None else tn
    grid = (n // tn, m // tm)
    in_specs = [pl.BlockSpec((tm, d), lambda j, i: (i, 0)),
                pl.BlockSpec((1, d), lambda j, i: (0, 0)),
                pl.BlockSpec((d, tn), lambda j, i: (0, j))]
    args = [x, g.reshape(1, d), w]
    if rope is not None:
        nblk = seq // tm
        for t in rope:
            in_specs.append(pl.BlockSpec((tm, LANES), lambda j, i: (i % nblk, 0)))
            args.append(t)
    return pl.pallas_call(
        functools.partial(_norm_matmul_kernel, rope=rope is not None, half=half, tn=tn),
        grid=grid, in_specs=in_specs,
        out_specs=pl.BlockSpec((tm, tn), lambda j, i: (i, j)),
        out_shape=jax.ShapeDtypeStruct((m, n), out_dtype),
        compiler_params=_cparams(("parallel", "parallel")),
        name="norm_matmul",
    )(*args)


def _mla_proj_kernel(x_ref, g_ref, win_ref, qn_ref, kvn_ref, wuq_ref, wuk_ref, wuv_ref,
                     c_ref, sp_ref, sm_ref, q_ref, k_ref, v_ref):
    half = MLA_ROPE // 2
    xn = _rms(x_ref[...], g_ref[...]).astype(BF16)
    c = _dot(xn, win_ref[...])
    cq = _rms(c[:, :MLA_Q_LORA], qn_ref[...]).astype(BF16)
    ckv = _rms(c[:, MLA_Q_LORA:MLA_Q_LORA + MLA_KV_LORA], kvn_ref[...]).astype(BF16)
    kr = c[:, MLA_Q_LORA + MLA_KV_LORA:]

    def rot(y, width):
        reps = width // LANES
        cc = jnp.tile(c_ref[...], (1, reps))
        sp = jnp.tile(sp_ref[...], (1, reps))
        sm = jnp.tile(sm_ref[...], (1, reps))
        return y * cc + pltpu.roll(y, half, axis=1) * sp + pltpu.roll(y, width - half, axis=1) * sm

    nq = wuq_ref.shape[1]
    q = rot(_dot(cq, wuq_ref[...]), nq)
    q_ref[...] = q.astype(q_ref.dtype)
    kr = rot(kr, LANES)
    k = _dot(ckv, wuk_ref[...]) + jnp.tile(kr, (1, nq // LANES))
    k_ref[...] = k.astype(k_ref.dtype)
    v_ref[...] = _dot(ckv, wuv_ref[...]).astype(v_ref.dtype)


def _mla_proj(x, g, win, qn, kvn, wuq, wuk, wuv, rope, seq, tm=512):
    m, d = x.shape
    nblk = seq // tm
    full = lambda a: pl.BlockSpec(a.shape, lambda i: (0, 0))
    qn2, kvn2, g2 = qn.reshape(1, -1), kvn.reshape(1, -1), g.reshape(1, -1)
    in_specs = [pl.BlockSpec((tm, d), lambda i: (i, 0)), full(g2), full(win), full(qn2), full(kvn2),
                full(wuq), full(wuk), full(wuv)]
    in_specs += [pl.BlockSpec((tm, LANES), lambda i: (i % nblk, 0))] * 3
    nq, nv = wuq.shape[1], wuv.shape[1]
    return pl.pallas_call(
        _mla_proj_kernel, grid=(m // tm,), in_specs=in_specs,
        out_specs=[pl.BlockSpec((tm, nq), lambda i: (i, 0)), pl.BlockSpec((tm, nq), lambda i: (i, 0)),
                   pl.BlockSpec((tm, nv), lambda i: (i, 0))],
        out_shape=[jax.ShapeDtypeStruct((m, nq), BF16), jax.ShapeDtypeStruct((m, nq), BF16),
                   jax.ShapeDtypeStruct((m, nv), BF16)],
        compiler_params=_cparams(("parallel",)),
        name="mla_proj",
    )(x, g2, win, qn2, kvn2, wuq, wuk, wuv, *rope)


def _out_proj_kernel(a_ref, w_ref, h_ref, o_ref):
    o_ref[...] = h_ref[...] + _dot(a_ref[...], w_ref[...])


def _out_proj(a, w, h, tm=512):
    m, k = a.shape
    d = w.shape[1]
    return pl.pallas_call(
        _out_proj_kernel, grid=(m // tm,),
        in_specs=[pl.BlockSpec((tm, k), lambda i: (i, 0)), pl.BlockSpec((k, d), lambda i: (0, 0)),
                  pl.BlockSpec((tm, d), lambda i: (i, 0))],
        out_specs=pl.BlockSpec((tm, d), lambda i: (i, 0)),
        out_shape=jax.ShapeDtypeStruct((m, d), F32),
        compiler_params=_cparams(("parallel",)),
        name="out_proj",
    )(a, w, h)


def _dil_out_proj_kernel(a0_ref, a1_ref, a2_ref, w_ref, h_ref, o_ref):
    xs = [a0_ref[...], a1_ref[...], a2_ref[...]]
    width = xs[0].shape[1]
    lane = lax.broadcasted_iota(jnp.int32, xs[0].shape, 1) % LANES
    is_out = lane < HEAD_DIM
    mx = jnp.maximum(jnp.maximum(xs[0], xs[1]), xs[2])
    ws = [jnp.exp(x - mx) for x in xs]
    den = ws[0] + ws[1] + ws[2]
    comb = jnp.zeros_like(xs[0])
    for x, wgt in zip(xs, ws):
        wn = pltpu.roll(wgt / den, width - HEAD_DIM, axis=1)
        comb = comb + wn * x
    comb = jnp.where(is_out, comb, 0.0).astype(BF16)
    o_ref[...] = h_ref[...] + _dot(comb, w_ref[...])


def _dil_out_proj(a_list, w_exp, h, tm=256):
    m, k = a_list[0].shape
    d = w_exp.shape[1]
    a_spec = pl.BlockSpec((tm, k), lambda i: (i, 0))
    return pl.pallas_call(
        _dil_out_proj_kernel, grid=(m // tm,),
        in_specs=[a_spec, a_spec, a_spec, pl.BlockSpec((k, d), lambda i: (0, 0)),
                  pl.BlockSpec((tm, d), lambda i: (i, 0))],
        out_specs=pl.BlockSpec((tm, d), lambda i: (i, 0)),
        out_shape=jax.ShapeDtypeStruct((m, d), F32),
        compiler_params=_cparams(("parallel",)),
        name="dil_out_proj",
    )(*a_list, w_exp, h)


def _gelu_exact(x):
    return 0.5 * x * (1.0 + lax.erf(x * (1.0 / math.sqrt(2.0))))


def _ffn_kernel(x_ref, g_ref, wg_ref, wu_ref, wd_ref, cw_ref, cb_ref, o_ref, xn_ref, acc_ref, carry_ref,
                *, nblk, fc, final_norm):
    i, c = pl.program_id(0), pl.program_id(1)
    nc = pl.num_programs(1)
    tm = x_ref.shape[0]

    @pl.when(c == 0)
    def _():
        xn_ref[...] = _rms(x_ref[...], g_ref[...]).astype(BF16)
        acc_ref[...] = jnp.zeros_like(acc_ref)

    @pl.when(jnp.logical_and(c == 0, i % nblk == 0))
    def _():
        carry_ref[...] = jnp.zeros_like(carry_ref)

    xn = xn_ref[...]
    gate = _dot(xn, wg_ref[...])
    up = _dot(xn, wu_ref[...])
    col = pl.multiple_of(c * fc, LANES)
    prev = carry_ref[:, pl.ds(col, fc)]
    row = lax.broadcasted_iota(jnp.int32, gate.shape, 0)
    p1, p2 = prev[7:8, :], prev[6:7, :]
    s1 = jnp.where(row == 0, p1, pltpu.roll(gate, 1, axis=0))
    s2 = jnp.where(row == 0, p2, jnp.where(row == 1, p1, pltpu.roll(gate, 2, axis=0)))
    carry_ref[:, pl.ds(col, fc)] = gate[tm - 8:, :]
    cw = cw_ref[0]
    gc = cb_ref[0] + cw[0:1, :] * s2 + cw[1:2, :] * s1 + cw[2:3, :] * gate
    hidden = (_gelu_exact(gc) * up).astype(BF16)
    acc_ref[...] += _dot(hidden, wd_ref[...])

    @pl.when(c == nc - 1)
    def _():
        o_ref[...] = x_ref[...] + acc_ref[...]


def _ffn(h, g, wg, wu, wd, cw, cb, seq, tm=512, fc=1408):
    m, d = h.shape
    f = wg.shape[1]
    nc = f // fc
    cw3 = cw.reshape(CONV_WIDTH, nc, fc).transpose(1, 0, 2)
    cb3 = cb.reshape(nc, 1, fc)
    return pl.pallas_call(
        functools.partial(_ffn_kernel, nblk=seq // tm, fc=fc, final_norm=False),
        grid=(m // tm, nc),
        in_specs=[pl.BlockSpec((tm, d), lambda i, c: (i, 0)),
                  pl.BlockSpec((1, d), lambda i, c: (0, 0)),
                  pl.BlockSpec((d, fc), lambda i, c: (0, c)),
                  pl.BlockSpec((d, fc), lambda i, c: (0, c)),
                  pl.BlockSpec((fc, d), lambda i, c: (c, 0)),
                  pl.BlockSpec((1, CONV_WIDTH, fc), lambda i, c: (c, 0, 0)),
                  pl.BlockSpec((1, 1, fc), lambda i, c: (c, 0, 0))],
        out_specs=pl.BlockSpec((tm, d), lambda i, c: (i, 0)),
        out_shape=jax.ShapeDtypeStruct((m, d), F32),
        scratch_shapes=[pltpu.VMEM((tm, d), BF16), pltpu.VMEM((tm, d), F32), pltpu.VMEM((8, f), F32)],
        compiler_params=_cparams(("arbitrary", "arbitrary")),
        name="conv_ffn",
    )(h, g.reshape(1, d), wg, wu, wd, cw3, cb3)


def _final_norm_kernel(x_ref, g_ref, o_ref):
    o_ref[...] = _rms(x_ref[...], g_ref[...])


def _final_norm(h, g, tm=1024):
    m, d = h.shape
    return pl.pallas_call(
        _final_norm_kernel, grid=(m // tm,),
        in_specs=[pl.BlockSpec((tm, d), lambda i: (i, 0)), pl.BlockSpec((1, d), lambda i: (0, 0))],
        out_specs=pl.BlockSpec((tm, d), lambda i: (i, 0)),
        out_shape=jax.ShapeDtypeStruct((m, d), F32),
        compiler_params=_cparams(("parallel",)),
        name="final_norm",
    )(h, g.reshape(1, d))


def _flash_kernel(q_ref, k_ref, vt_ref, o_ref, sa_ref, sb_ref, pa_ref, pb_ref, acc_ref, *moba_refs, tq, tk, c, moba):
    i = pl.program_id(2)
    q = q_ref[0, 0]
    dv = vt_ref.shape[2]
    ones = jnp.ones((ONES_ROWS, tk), BF16)
    krow = lax.broadcasted_iota(jnp.int32, (tk, tq), 0)
    qcol = lax.broadcasted_iota(jnp.int32, (tk, tq), 1)

    if moba:
        kmean_ref, sel_ref = moba_refs
        nb = kmean_ref.shape[0]

        @pl.when(i == 0)
        def _():
            for n in range(nb):
                rows = k_ref[0, 0, n * tk:(n + 1) * tk, :].astype(F32)
                kmean_ref[n:n + 1, :] = jnp.sum(rows, axis=0, keepdims=True) * (1.0 / tk)

        km = kmean_ref[...]
        km_hi = km.astype(BF16)
        km_lo = (km - km_hi.astype(F32)).astype(BF16)
        gate = _dot_nt(km_hi, q) + _dot_nt(km_lo, q)
        blk = lax.broadcasted_iota(jnp.int32, gate.shape, 0)
        lane = lax.broadcasted_iota(jnp.int32, gate.shape, 1)
        qblk = i * (tq // tk)
        for t in range(1, tq // tk):
            qblk = qblk + jnp.where(lane >= t * tk, 1, 0)
        past = blk < qblk
        blk_f = blk.astype(F32)
        g = jnp.where(past, gate, NEG_INF)
        picked = jnp.zeros(gate.shape, F32)
        for _ in range(MOBA_TOPK):
            mx = jnp.max(g, axis=0, keepdims=True)
            first = jnp.min(jnp.where(g == mx, blk_f, float(nb)), axis=0, keepdims=True)
            hit = blk_f == first
            picked = jnp.where(hit, 1.0, picked)
            g = jnp.where(hit, -jnp.inf, g)
        sel_ref[...] = jnp.where(past, picked, 0.0)

    def mask(j, s, off):
        if off is None:
            if not moba:
                return s
            return jnp.where(sel_ref[pl.ds(j, 1), :] > 0.5, s, NEG_INF)
        causal = krow + off <= qcol
        if not moba:
            return jnp.where(causal, s, NEG_INF)
        same = jnp.where(jnp.logical_and(qcol >= off, qcol < off + tk), 1.0, 0.0)
        later = jnp.where(qcol >= off + tk, sel_ref[pl.ds(j, 1), :], 0.0)
        allowed = jnp.where(same > 0.5, jnp.where(causal, 1.0, 0.0), later)
        return jnp.where(allowed > 0.5, s, NEG_INF)

    def qk(j):
        start = pl.multiple_of(j * tk, tk)
        return _dot_nt(k_ref[0, 0, pl.ds(start, tk), :], q)

    def pv(j, p):
        start = pl.multiple_of(jnp.maximum(j, 0) * tk, tk)
        vt = jnp.concatenate([vt_ref[0, 0, :, pl.ds(start, tk)], ones], axis=0)
        return _dot(vt, p)

    def step(j, stats, s_in, s_out, p_in, p_out, off, last):
        m, alpha_prev = stats
        if not last:
            s_out[...] = qk(j + 1)
        pvr = pv(j - 1, p_in[...])
        s = mask(j, s_in[...], off)
        m_new = jnp.maximum(m, jnp.max(s, axis=0, keepdims=True))
        alpha = jnp.exp2((m - m_new) * c)
        p = jnp.exp2((s - m_new) * c)
        acc_ref[...] = alpha_prev * acc_ref[...] + pvr
        p_out[...] = p.astype(BF16)
        return m_new, alpha

    sa_ref[...] = qk(0)
    pa_ref[...] = jnp.zeros_like(pa_ref)
    acc_ref[...] = jnp.zeros_like(acc_ref)
    stats = (jnp.full((1, tq), NEG_INF, F32), jnp.ones((1, tq), F32))

    def body(jj, stats):
        stats = step(2 * jj, stats, sa_ref, sb_ref, pa_ref, pb_ref, None, False)
        return step(2 * jj + 1, stats, sb_ref, sa_ref, pb_ref, pa_ref, None, False)

    stats = lax.fori_loop(0, i, body, stats)
    n_full = 2 * i
    stats = step(n_full, stats, sa_ref, sb_ref, pa_ref, pb_ref, 0, False)
    _, alpha = step(n_full + 1, stats, sb_ref, sa_ref, pb_ref, pa_ref, tk, True)
    acc = alpha * acc_ref[...] + pv(n_full + 1, pa_ref[...])
    o_ref[0, 0] = (acc[:dv] / acc[dv:dv + 1]).astype(o_ref.dtype)


def _flash_attention(q, k, vt, scale, moba):
    b, h, s, dk = q.shape
    dv = vt.shape[2]
    tq, tk = FLASH_TQ, FLASH_TK
    assert tq == 2 * tk and s % tq == 0 and (not moba or tk == MOBA_BLOCK)
    scratch = [pltpu.VMEM((tk, tq), F32), pltpu.VMEM((tk, tq), F32), pltpu.VMEM((tk, tq), BF16),
               pltpu.VMEM((tk, tq), BF16), pltpu.VMEM((dv + ONES_ROWS, tq), F32)]
    if moba:
        scratch += [pltpu.VMEM((s // tk, dk), F32), pltpu.VMEM((s // tk, tq), F32)]
    return pl.pallas_call(
        functools.partial(_flash_kernel, tq=tq, tk=tk, c=scale * LOG2E, moba=moba),
        grid=(b, h, s // tq),
        in_specs=[pl.BlockSpec((1, 1, tq, dk), lambda b_, h_, i: (b_, h_, i, 0)),
                  pl.BlockSpec((1, 1, s, dk), lambda b_, h_, i: (b_, h_, 0, 0)),
                  pl.BlockSpec((1, 1, dv, s), lambda b_, h_, i: (b_, h_, 0, 0))],
        out_specs=pl.BlockSpec((1, 1, dv, tq), lambda b_, h_, i: (b_, h_, 0, i)),
        out_shape=jax.ShapeDtypeStruct((b, h, dv, s), BF16),
        scratch_shapes=scratch,
        compiler_params=_cparams(("parallel", "parallel", "arbitrary")),
        name="moba_attention" if moba else "mla_attention",
    )(q, k, vt)


def _sb_kernel(q_ref, k_ref, v_ref, o_ref, acc_ref, stay_ref, *, t, nh):
    i = pl.program_id(2)
    r = lax.broadcasted_iota(jnp.int32, (t, t), 0)
    c = lax.broadcasted_iota(jnp.int32, (t, t), 1)
    later = jnp.where(r > c, 1.0, 0.0).astype(BF16)
    past = c < r

    def tile(j, diag):
        start = pl.multiple_of(j * t, t)
        top = None
        for hh in range(nh):
            q = q_ref[0, hh] * 0.125
            z = _dot_nt(q, k_ref[0, hh, pl.ds(start, t), :])
            lp = jnp.log1p(jnp.exp(-jnp.abs(z)))
            log_beta = jnp.minimum(z, 0.0) - lp
            log_stay = log_beta - z
            if diag:
                log_stay = jnp.where(past, log_stay, 0.0)
            hi = log_stay.astype(BF16)
            r1 = log_stay - hi.astype(F32)
            mid = r1.astype(BF16)
            lo = (r1 - mid.astype(F32)).astype(BF16)
            inner = _dot(hi, later) + _dot(mid, later) + _dot(lo, later)
            a = jnp.exp(log_beta + inner + stay_ref[hh])
            if diag:
                a = jnp.where(past, a, 0.0)
            acc_ref[hh] += _dot(a.astype(BF16), v_ref[0, hh, pl.ds(start, t), :])
            stay = stay_ref[hh] + jnp.sum(log_stay, axis=1, keepdims=True)
            stay_ref[hh] = stay
            mx = jnp.max(stay)
            top = mx if top is None else jnp.maximum(top, mx)
        return top

    acc_ref[...] = jnp.zeros_like(acc_ref)
    stay_ref[...] = jnp.zeros_like(stay_ref)
    top = tile(i, True)

    def cond(state):
        j, top = state
        return jnp.logical_and(j >= 0, top > SB_UNDERFLOW)

    def body(state):
        j, _ = state
        return j - 1, tile(j, False)

    lax.while_loop(cond, body, (i - 1, top))
    o_ref[0] = acc_ref[...].astype(o_ref.dtype)


def _sb_attention(q, k, v, t=256, nh=2):
    b, h, s, dh = q.shape
    return pl.pallas_call(
        functools.partial(_sb_kernel, t=t, nh=nh),
        grid=(b, h // nh, s // t),
        in_specs=[pl.BlockSpec((1, nh, t, dh), lambda b_, h_, i: (b_, h_, i, 0)),
                  pl.BlockSpec((1, nh, s, dh), lambda b_, h_, i: (b_, h_, 0, 0)),
                  pl.BlockSpec((1, nh, s, dh), lambda b_, h_, i: (b_, h_, 0, 0))],
        out_specs=pl.BlockSpec((1, nh, t, dh), lambda b_, h_, i: (b_, h_, i, 0)),
        out_shape=jax.ShapeDtypeStruct((b, h, s, dh), BF16),
        scratch_shapes=[pltpu.VMEM((nh, t, dh), F32), pltpu.VMEM((nh, t, 1), F32)],
        compiler_params=_cparams(("parallel", "parallel", "arbitrary")),
        name="sb_attention",
    )(q, k, v)


def _dil_kernel(q_ref, k_ref, v_ref, o_ref, *, nblk, nseq, unroll):
    t = DIL_BLOCK
    dh = q_ref.shape[-1]
    r = lax.broadcasted_iota(jnp.int32, (t, t), 0)
    c = lax.broadcasted_iota(jnp.int32, (t, t), 1)

    def block(g, n):
        start = pl.multiple_of(n * t, t)
        prev = pl.multiple_of(jnp.maximum(n - 1, 0) * t, t)
        q = q_ref[g, pl.ds(start, t), :] * 0.125
        s_cur = jnp.where(c <= r, _dot_nt(q, k_ref[g, pl.ds(start, t), :]), NEG_INF)
        first = jnp.where(n > 0, 0, t)
        s_prev = jnp.where(c >= r + first, _dot_nt(q, k_ref[g, pl.ds(prev, t), :]), NEG_INF)
        m = jnp.maximum(jnp.max(s_cur, axis=1, keepdims=True), jnp.max(s_prev, axis=1, keepdims=True))
        p_cur = jnp.exp(s_cur - m)
        p_prev = jnp.exp(s_prev - m)
        l = jnp.sum(p_cur, axis=1, keepdims=True) + jnp.sum(p_prev, axis=1, keepdims=True)
        acc = _dot(p_cur.astype(BF16), v_ref[g, pl.ds(start, t), :])
        acc = acc + _dot(p_prev.astype(BF16), v_ref[g, pl.ds(prev, t), :])
        lse = m + jnp.log(l)
        o_ref[g, pl.ds(start, t), :] = jnp.concatenate([acc / l, jnp.broadcast_to(lse, (t, dh))], axis=1)

    def body(nn, _):
        for u in range(unroll):
            for g in range(nseq):
                block(g, nn * unroll + u)
        return 0

    lax.fori_loop(0, nblk // unroll, body, 0)


def _dil_attention(q, k, v):
    ntot, length, dh = q.shape
    nblk = length // DIL_BLOCK
    nseq = max(1, DIL_ILP // nblk)
    unroll = DIL_ILP // nseq
    spec = pl.BlockSpec((nseq, length, dh), lambda n: (n, 0, 0))
    return pl.pallas_call(
        functools.partial(_dil_kernel, nblk=nblk, nseq=nseq, unroll=unroll),
        grid=(ntot // nseq,),
        in_specs=[spec, spec, spec],
        out_specs=pl.BlockSpec((nseq, length, 2 * dh), lambda n: (n, 0, 0)),
        out_shape=jax.ShapeDtypeStruct((ntot, length, 2 * dh), F32),
        compiler_params=_cparams(("parallel",)),
        name="dil_attention",
    )(q, k, v)


def _rope_tables(seq, dim, first_lane, period):
    half = dim // 2
    inv = ROPE_THETA ** (-jnp.arange(0, dim, 2, dtype=F32) / dim)
    ang = jnp.arange(seq, dtype=F32)[:, None] * inv[None, :]
    cos, sin = jnp.cos(ang), jnp.sin(ang)
    zeros = jnp.zeros_like(sin)
    pad_lo = jnp.zeros((seq, first_lane), F32)
    pad_hi = jnp.zeros((seq, period - first_lane - dim), F32)
    c = jnp.concatenate([pad_lo + 1.0, cos, cos, pad_hi + 1.0], axis=1)
    s_plus = jnp.concatenate([pad_lo, zeros, sin, pad_hi], axis=1)
    s_minus = jnp.concatenate([pad_lo, -sin, zeros, pad_hi], axis=1)
    reps = LANES // period
    return tuple(jnp.tile(t, (1, reps)) for t in (c, s_plus, s_minus))


def _heads(y, b, s, nh, dh):
    return y.reshape(b, s, nh, dh).transpose(0, 2, 1, 3)


def _unheads(o):
    b, h, s, dh = o.shape
    return o.transpose(0, 2, 1, 3).reshape(b * s, h * dh)


def _heads_t(y, b, s, nh, dh):
    return y.reshape(b, s, nh, dh).transpose(0, 2, 3, 1)


def _unheads_t(o):
    b, h, dh, s = o.shape
    return o.transpose(0, 3, 1, 2).reshape(b * s, h * dh)


def _mla_weights(w_in, w_uq, w_ukv):
    d = w_in.shape[0]
    ql, kvl = MLA_Q_LORA, MLA_KV_LORA
    w_kr = jnp.zeros((d, LANES), F32).at[:, MLA_NOPE:MLA_NOPE + MLA_ROPE].set(w_in[:, ql + kvl:])
    win = jnp.concatenate([w_in[:, :ql + kvl], w_kr], axis=1)
    wuq = w_uq.reshape(ql, N_HEADS, MLA_NOPE + MLA_ROPE)
    wuq = jnp.pad(wuq, ((0, 0), (0, 0), (0, MLA_QK_PAD - MLA_NOPE - MLA_ROPE))).reshape(ql, N_HEADS * MLA_QK_PAD)
    wukv = w_ukv.reshape(kvl, N_HEADS, MLA_NOPE + MLA_V)
    wuk = jnp.pad(wukv[:, :, :MLA_NOPE], ((0, 0), (0, 0), (0, MLA_QK_PAD - MLA_NOPE))).reshape(kvl, N_HEADS * MLA_QK_PAD)
    wuv = wukv[:, :, MLA_NOPE:].reshape(kvl, N_HEADS * MLA_V)
    return win.astype(BF16), wuq.astype(BF16), wuk.astype(BF16), wuv.astype(BF16)


def kernel(x, norm_mix, norm_ffn, norm_final, a_w_qkv, a_w_o, b_w_in, b_q_norm, b_w_uq, b_kv_norm, b_w_ukv, b_w_o,
           c_w_qkv, c_w_o, d_w_qkv, d_w_o, ffn_w_gate, ffn_conv_w, ffn_conv_b, ffn_w_up, ffn_w_down):
    b, s, d = x.shape
    m = b * s
    hd = N_HEADS * HEAD_DIM
    depth = norm_mix.shape[0]
    rope_h = _rope_tables(s, HEAD_DIM, 0, HEAD_DIM)
    rope_r = _rope_tables(s, MLA_ROPE, MLA_NOPE, MLA_QK_PAD)
    h = x.reshape(m, d)
    for i in range(depth):
        kind, j = i % 4, i // 4
        g = norm_mix[i]
        if kind == 0:
            w = a_w_qkv[j].astype(BF16)
            qk = _norm_matmul(h, g, w[:, :2 * hd], s, rope=rope_h, half=HEAD_DIM // 2)
            v = _norm_matmul(h, g, w[:, 2 * hd:], s)
            q = _heads(qk[:, :hd], b, s, N_HEADS, HEAD_DIM)
            k = _heads(qk[:, hd:], b, s, N_HEADS, HEAD_DIM)
            o = _flash_attention(q, k, _heads_t(v, b, s, N_HEADS, HEAD_DIM), HEAD_DIM ** -0.5, True)
            h = _out_proj(_unheads_t(o), a_w_o[j].astype(BF16), h)
        elif kind == 1:
            win, wuq, wuk, wuv = _mla_weights(b_w_in[j], b_w_uq[j], b_w_ukv[j])
            q, k, v = _mla_proj(h, g, win, b_q_norm[j], b_kv_norm[j], wuq, wuk, wuv, rope_r, s)
            o = _flash_attention(_heads(q, b, s, N_HEADS, MLA_QK_PAD), _heads(k, b, s, N_HEADS, MLA_QK_PAD),
                                 _heads_t(v, b, s, N_HEADS, MLA_V), (MLA_NOPE + MLA_ROPE) ** -0.5, False)
            h = _out_proj(_unheads_t(o), b_w_o[j].astype(BF16), h)
        elif kind == 2:
            ng = len(DIL_GROUPS)
            w = c_w_qkv[j].astype(BF16).reshape(d, ng, 3, hd)
            qk = _norm_matmul(h, g, w[:, :, :2].reshape(d, ng * 2 * hd), s, rope=rope_h, half=HEAD_DIM // 2,
                              tn=2 * hd)
            v = _norm_matmul(h, g, w[:, :, 2].reshape(d, ng * hd), s)
            qk = qk.reshape(b, s, ng, 2, N_HEADS, HEAD_DIM)
            v = v.reshape(b, s, ng, N_HEADS, HEAD_DIM)
            outs = []
            for gi, (window, dil) in enumerate(DIL_GROUPS):
                length = s // dil

                def dilate(t):
                    return t.reshape(b, length, dil, N_HEADS, HEAD_DIM).transpose(0, 3, 2, 1, 4).reshape(
                        b * N_HEADS * dil, length, HEAD_DIM)

                og = _dil_attention(dilate(qk[:, :, gi, 0]), dilate(qk[:, :, gi, 1]), dilate(v[:, :, gi]))
                og = og.reshape(b, N_HEADS, dil, length, 2 * HEAD_DIM).transpose(0, 3, 2, 1, 4)
                outs.append(og.reshape(m, N_HEADS * 2 * HEAD_DIM))
            w_o = c_w_o[j].astype(BF16).reshape(N_HEADS, HEAD_DIM, d)
            w_o = jnp.pad(w_o, ((0, 0), (0, HEAD_DIM), (0, 0))).reshape(N_HEADS * 2 * HEAD_DIM, d)
            h = _dil_out_proj(outs, w_o, h)
        else:
            w = d_w_qkv[j].astype(BF16)
            qkv = _norm_matmul(h, g, w, s).reshape(b, s, 3, N_HEADS, HEAD_DIM)
            o = _sb_attention(qkv[:, :, 0].transpose(0, 2, 1, 3), qkv[:, :, 1].transpose(0, 2, 1, 3),
                              qkv[:, :, 2].transpose(0, 2, 1, 3))
            h = _out_proj(_unheads(o), d_w_o[j].astype(BF16), h)
        h = _ffn(h, norm_ffn[i], ffn_w_gate[i].astype(BF16), ffn_w_up[i].astype(BF16), ffn_w_down[i].astype(BF16),
                 ffn_conv_w[i], ffn_conv_b[i], s)
    return _final_norm(h, norm_final).reshape(b, s, d)
```

```python
import functools
import math

import jax
import jax.numpy as jnp
from jax import lax
from jax.experimental import pallas as pl
from jax.experimental.pallas import tpu as pltpu

D_MODEL = 1024
HEAD_DIM = 64
N_HEADS = 16
ROPE_THETA = 10000.0
RMS_EPS = 1e-6
NEG_INF = -1e30
MOBA_BLOCK = 256
MOBA_TOPK = 3
MLA_Q_LORA = 384
MLA_KV_LORA = 256
MLA_NOPE = 64
MLA_ROPE = 32
MLA_V = 64
MLA_QK_PAD = 128
DIL_GROUPS = ((128, 1), (512, 4), (2048, 16))
DIL_BLOCK = 128
D_FF = 2816
CONV_WIDTH = 3
LANES = 128
SB_UNDERFLOW = -104.0
VMEM_LIMIT = 56 * 1024 * 1024
LOG2E = 1.4426950408889634
FLASH_TQ = 512
FLASH_TK = 256
ONES_ROWS = 16
DIL_UNROLL = 2

F32 = jnp.float32
BF16 = jnp.bfloat16


def _cparams(sem):
    return pltpu.CompilerParams(dimension_semantics=sem, vmem_limit_bytes=VMEM_LIMIT)


def _dot(a, b):
    return jnp.dot(a, b, preferred_element_type=F32)


def _dot_nt(a, b):
    return lax.dot_general(a, b, (((1,), (1,)), ((), ())), preferred_element_type=F32)


def _rms(x, g):
    return x * lax.rsqrt(jnp.mean(x * x, axis=-1, keepdims=True) + RMS_EPS) * g


def _norm_matmul_kernel(*refs, rope, half, tn):
    if rope:
        x_ref, g_ref, w_ref, c_ref, sp_ref, sm_ref, o_ref = refs
    else:
        x_ref, g_ref, w_ref, o_ref = refs
    xn = _rms(x_ref[...], g_ref[...]).astype(BF16)
    y = _dot(xn, w_ref[...])
    if rope:
        reps = tn // LANES
        c = jnp.tile(c_ref[...], (1, reps))
        sp = jnp.tile(sp_ref[...], (1, reps))
        sm = jnp.tile(sm_ref[...], (1, reps))
        y = y * c + pltpu.roll(y, half, axis=1) * sp + pltpu.roll(y, tn - half, axis=1) * sm
    o_ref[...] = y.astype(o_ref.dtype)


def _norm_matmul(x, g, w, seq, *, rope=None, half=0, out_dtype=BF16, tm=512, tn=None):
    m, d = x.shape
    n = w.shape[1]
    tn = n if tn is None else tn
    grid = (n // tn, m // tm)
    in_specs = [pl.BlockSpec((tm, d), lambda j, i: (i, 0)),
                pl.BlockSpec((1, d), lambda j, i: (0, 0)),
                pl.BlockSpec((d, tn), lambda j, i: (0, j))]
    args = [x, g.reshape(1, d), w]
    if rope is not None:
        nblk = seq // tm
        for t in rope:
            in_specs.append(pl.BlockSpec((tm, LANES), lambda j, i: (i % nblk, 0)))
            args.append(t)
    return pl.pallas_call(
        functools.partial(_norm_matmul_kernel, rope=rope is not None, half=half, tn=tn),
        grid=grid, in_specs=in_specs,
        out_specs=pl.BlockSpec((tm, tn), lambda j, i: (i, j)),
        out_shape=jax.ShapeDtypeStruct((m, n), out_dtype),
        compiler_params=_cparams(("parallel", "parallel")),
        name="norm_matmul",
    )(*args)


def _norm_matmul_t_kernel(x_ref, g_ref, wt_ref, o_ref):
    xn = _rms(x_ref[...], g_ref[...]).astype(BF16)
    o_ref[0] = _dot_nt(wt_ref[...], xn).astype(o_ref.dtype)


def _norm_matmul_t(x, g, wt, b, seq, tm=512):
    m, d = x.shape
    n = wt.shape[0]
    nblk = seq // tm
    return pl.pallas_call(
        _norm_matmul_t_kernel, grid=(m // tm,),
        in_specs=[pl.BlockSpec((tm, d), lambda i: (i, 0)), pl.BlockSpec((1, d), lambda i: (0, 0)),
                  pl.BlockSpec((n, d), lambda i: (0, 0))],
        out_specs=pl.BlockSpec((1, n, tm), lambda i: (i // nblk, 0, i % nblk)),
        out_shape=jax.ShapeDtypeStruct((b, n, seq), BF16),
        compiler_params=_cparams(("parallel",)),
        name="norm_matmul_t",
    )(x, g.reshape(1, d), wt)


def _mla_proj_kernel(x_ref, g_ref, win_ref, qn_ref, kvn_ref, wuq_ref, wuk_ref, wuv_ref,
                     c_ref, sp_ref, sm_ref, q_ref, k_ref, v_ref):
    half = MLA_ROPE // 2
    xn = _rms(x_ref[...], g_ref[...]).astype(BF16)
    c = _dot(xn, win_ref[...])
    cq = _rms(c[:, :MLA_Q_LORA], qn_ref[...]).astype(BF16)
    ckv = _rms(c[:, MLA_Q_LORA:MLA_Q_LORA + MLA_KV_LORA], kvn_ref[...]).astype(BF16)
    kr = c[:, MLA_Q_LORA + MLA_KV_LORA:]

    def rot(y, width):
        reps = width // LANES
        cc = jnp.tile(c_ref[...], (1, reps))
        sp = jnp.tile(sp_ref[...], (1, reps))
        sm = jnp.tile(sm_ref[...], (1, reps))
        return y * cc + pltpu.roll(y, half, axis=1) * sp + pltpu.roll(y, width - half, axis=1) * sm

    nq = wuq_ref.shape[1]
    q = rot(_dot(cq, wuq_ref[...]), nq)
    q_ref[...] = q.astype(q_ref.dtype)
    kr = rot(kr, LANES)
    k = _dot(ckv, wuk_ref[...]) + jnp.tile(kr, (1, nq // LANES))
    k_ref[...] = k.astype(k_ref.dtype)
    v_ref[0] = _dot_nt(wuv_ref[...], ckv).astype(v_ref.dtype)


def _mla_proj(x, g, win, qn, kvn, wuq, wuk, wuv, rope, b, seq, tm=512):
    m, d = x.shape
    nblk = seq // tm
    full = lambda a: pl.BlockSpec(a.shape, lambda i: (0, 0))
    qn2, kvn2, g2 = qn.reshape(1, -1), kvn.reshape(1, -1), g.reshape(1, -1)
    in_specs = [pl.BlockSpec((tm, d), lambda i: (i, 0)), full(g2), full(win), full(qn2), full(kvn2),
                full(wuq), full(wuk), full(wuv)]
    in_specs += [pl.BlockSpec((tm, LANES), lambda i: (i % nblk, 0))] * 3
    nq, nv = wuq.shape[1], wuv.shape[0]
    return pl.pallas_call(
        _mla_proj_kernel, grid=(m // tm,), in_specs=in_specs,
        out_specs=[pl.BlockSpec((tm, nq), lambda i: (i, 0)), pl.BlockSpec((tm, nq), lambda i: (i, 0)),
                   pl.BlockSpec((1, nv, tm), lambda i: (i // nblk, 0, i % nblk))],
        out_shape=[jax.ShapeDtypeStruct((m, nq), BF16), jax.ShapeDtypeStruct((m, nq), BF16),
                   jax.ShapeDtypeStruct((b, nv, seq), BF16)],
        compiler_params=_cparams(("parallel",)),
        name="mla_proj",
    )(x, g2, win, qn2, kvn2, wuq, wuk, wuv, *rope)


def _out_proj_kernel(a_ref, w_ref, h_ref, o_ref):
    o_ref[...] = h_ref[...] + _dot(a_ref[...], w_ref[...])


def _out_proj(a, w, h, tm=512):
    m, k = a.shape
    d = w.shape[1]
    return pl.pallas_call(
        _out_proj_kernel, grid=(m // tm,),
        in_specs=[pl.BlockSpec((tm, k), lambda i: (i, 0)), pl.BlockSpec((k, d), lambda i: (0, 0)),
                  pl.BlockSpec((tm, d), lambda i: (i, 0))],
        out_specs=pl.BlockSpec((tm, d), lambda i: (i, 0)),
        out_shape=jax.ShapeDtypeStruct((m, d), F32),
        compiler_params=_cparams(("parallel",)),
        name="out_proj",
    )(a, w, h)


def _gelu_exact(x):
    return 0.5 * x * (1.0 + lax.erf(x * (1.0 / math.sqrt(2.0))))


def _ffn_kernel(x_ref, g_ref, wg_ref, wu_ref, wd_ref, cw_ref, cb_ref, o_ref, xn_ref, acc_ref, carry_ref,
                *, nblk, fc, final_norm):
    i, c = pl.program_id(0), pl.program_id(1)
    nc = pl.num_programs(1)
    tm = x_ref.shape[0]

    @pl.when(c == 0)
    def _():
        xn_ref[...] = _rms(x_ref[...], g_ref[...]).astype(BF16)
        acc_ref[...] = jnp.zeros_like(acc_ref)

    @pl.when(jnp.logical_and(c == 0, i % nblk == 0))
    def _():
        carry_ref[...] = jnp.zeros_like(carry_ref)

    xn = xn_ref[...]
    gate = _dot(xn, wg_ref[...])
    up = _dot(xn, wu_ref[...])
    col = pl.multiple_of(c * fc, LANES)
    prev = carry_ref[:, pl.ds(col, fc)]
    row = lax.broadcasted_iota(jnp.int32, gate.shape, 0)
    p1, p2 = prev[7:8, :], prev[6:7, :]
    s1 = jnp.where(row == 0, p1, pltpu.roll(gate, 1, axis=0))
    s2 = jnp.where(row == 0, p2, jnp.where(row == 1, p1, pltpu.roll(gate, 2, axis=0)))
    carry_ref[:, pl.ds(col, fc)] = gate[tm - 8:, :]
    cw = cw_ref[0]
    gc = cb_ref[0] + cw[0:1, :] * s2 + cw[1:2, :] * s1 + cw[2:3, :] * gate
    hidden = (_gelu_exact(gc) * up).astype(BF16)
    acc_ref[...] += _dot(hidden, wd_ref[...])

    @pl.when(c == nc - 1)
    def _():
        o_ref[...] = x_ref[...] + acc_ref[...]


def _ffn(h, g, wg, wu, wd, cw, cb, seq, tm=512, fc=1408):
    m, d = h.shape
    f = wg.shape[1]
    nc = f // fc
    cw3 = cw.reshape(CONV_WIDTH, nc, fc).transpose(1, 0, 2)
    cb3 = cb.reshape(nc, 1, fc)
    return pl.pallas_call(
        functools.partial(_ffn_kernel, nblk=seq // tm, fc=fc, final_norm=False),
        grid=(m // tm, nc),
        in_specs=[pl.BlockSpec((tm, d), lambda i, c: (i, 0)),
                  pl.BlockSpec((1, d), lambda i, c: (0, 0)),
                  pl.BlockSpec((d, fc), lambda i, c: (0, c)),
                  pl.BlockSpec((d, fc), lambda i, c: (0, c)),
                  pl.BlockSpec((fc, d), lambda i, c: (c, 0)),
                  pl.BlockSpec((1, CONV_WIDTH, fc), lambda i, c: (c, 0, 0)),
                  pl.BlockSpec((1, 1, fc), lambda i, c: (c, 0, 0))],
        out_specs=pl.BlockSpec((tm, d), lambda i, c: (i, 0)),
        out_shape=jax.ShapeDtypeStruct((m, d), F32),
        scratch_shapes=[pltpu.VMEM((tm, d), BF16), pltpu.VMEM((tm, d), F32), pltpu.VMEM((8, f), F32)],
        compiler_params=_cparams(("arbitrary", "arbitrary")),
        name="conv_ffn",
    )(h, g.reshape(1, d), wg, wu, wd, cw3, cb3)


def _final_norm_kernel(x_ref, g_ref, o_ref):
    o_ref[...] = _rms(x_ref[...], g_ref[...])


def _final_norm(h, g, tm=1024):
    m, d = h.shape
    return pl.pallas_call(
        _final_norm_kernel, grid=(m // tm,),
        in_specs=[pl.BlockSpec((tm, d), lambda i: (i, 0)), pl.BlockSpec((1, d), lambda i: (0, 0))],
        out_specs=pl.BlockSpec((tm, d), lambda i: (i, 0)),
        out_shape=jax.ShapeDtypeStruct((m, d), F32),
        compiler_params=_cparams(("parallel",)),
        name="final_norm",
    )(h, g.reshape(1, d))


def _split_heads(x2):
    x = x2.astype(F32)
    low = lax.broadcasted_iota(jnp.int32, x.shape, 1) < HEAD_DIM
    return [jnp.where(low, x, 0.0).astype(BF16), jnp.where(low, 0.0, x).astype(BF16)]


def _flash_kernel(q_ref, k_ref, vt_ref, o_ref, s_ref, p_ref, acc_ref, *moba_refs, tq, tk, c, moba):
    i = pl.program_id(2)
    heads = range(2)
    dv = vt_ref.shape[1] // 2
    ones = jnp.ones((ONES_ROWS, tk), BF16)
    krow = lax.broadcasted_iota(jnp.int32, (tk, tq), 0)
    qcol = lax.broadcasted_iota(jnp.int32, (tk, tq), 1)

    if moba:
        kmean_ref, sel_ref = moba_refs
        nb = kmean_ref.shape[0]
        qs = _split_heads(q_ref[0])

        def ktile(start, hh):
            return k_ref[0, pl.ds(start, tk), :]

        @pl.when(i == 0)
        def _():
            for n in range(nb):
                rows = k_ref[0, n * tk:(n + 1) * tk, :].astype(F32)
                kmean_ref[n:n + 1, :] = jnp.sum(rows, axis=0, keepdims=True) * (1.0 / tk)

        km = kmean_ref[...]
        km_hi = km.astype(BF16)
        km_lo = (km - km_hi.astype(F32)).astype(BF16)
        blk = lax.broadcasted_iota(jnp.int32, (nb, tq), 0)
        lane = lax.broadcasted_iota(jnp.int32, (nb, tq), 1)
        qblk = i * (tq // tk)
        for t in range(1, tq // tk):
            qblk = qblk + jnp.where(lane >= t * tk, 1, 0)
        past = blk < qblk
        blk_f = blk.astype(F32)
        for hh in heads:
            gate = _dot_nt(km_hi, qs[hh]) + _dot_nt(km_lo, qs[hh])
            g = jnp.where(past, gate, NEG_INF)
            picked = jnp.zeros(gate.shape, F32)
            for _ in range(MOBA_TOPK):
                mx = jnp.max(g, axis=0, keepdims=True)
                first = jnp.min(jnp.where(g == mx, blk_f, float(nb)), axis=0, keepdims=True)
                hit = blk_f == first
                picked = jnp.where(hit, 1.0, picked)
                g = jnp.where(hit, -jnp.inf, g)
            sel_ref[hh] = jnp.where(past, picked, 0.0)
    else:
        dk = q_ref.shape[2] // 2
        qs = [q_ref[0, :, hh * dk:(hh + 1) * dk] for hh in heads]

        def ktile(start, hh):
            return k_ref[0, pl.ds(start, tk), hh * dk:(hh + 1) * dk]

    def mask(j, hh, s, off):
        if off is None:
            if not moba:
                return s
            return jnp.where(sel_ref[hh, pl.ds(j, 1), :] > 0.5, s, NEG_INF)
        causal = krow + off <= qcol
        if not moba:
            return jnp.where(causal, s, NEG_INF)
        same = jnp.where(jnp.logical_and(qcol >= off, qcol < off + tk), 1.0, 0.0)
        later = jnp.where(qcol >= off + tk, sel_ref[hh, pl.ds(j, 1), :], 0.0)
        allowed = jnp.where(same > 0.5, jnp.where(causal, 1.0, 0.0), later)
        return jnp.where(allowed > 0.5, s, NEG_INF)

    def qk(j, hh):
        start = pl.multiple_of(j * tk, tk)
        return _dot_nt(ktile(start, hh), qs[hh])

    def pv(j, hh, p):
        start = pl.multiple_of(jnp.maximum(j, 0) * tk, tk)
        vt = jnp.concatenate([vt_ref[0, hh * dv:(hh + 1) * dv, pl.ds(start, tk)], ones], axis=0)
        return _dot(vt, p)

    def step(j, stats, cur, off, last):
        new = []
        for hh in heads:
            m, alpha_prev = stats[hh]
            if not last:
                s_ref[hh, 1 - cur] = qk(j + 1, hh)
            pvr = pv(j - 1, hh, p_ref[hh, cur])
            s = mask(j, hh, s_ref[hh, cur], off)
            m_new = jnp.maximum(m, jnp.max(s, axis=0, keepdims=True))
            alpha = jnp.exp2((m - m_new) * c)
            p = jnp.exp2((s - m_new) * c)
            acc_ref[hh] = alpha_prev * acc_ref[hh] + pvr
            p_ref[hh, 1 - cur] = p.astype(BF16)
            new.append((m_new, alpha))
        return tuple(new)

    for hh in heads:
        s_ref[hh, 0] = qk(0, hh)
        p_ref[hh, 0] = jnp.zeros((tk, tq), BF16)
    acc_ref[...] = jnp.zeros_like(acc_ref)
    init = (jnp.full((1, tq), NEG_INF, F32), jnp.ones((1, tq), F32))
    stats = (init, init)

    def body(jj, stats):
        stats = step(2 * jj, stats, 0, None, False)
        return step(2 * jj + 1, stats, 1, None, False)

    stats = lax.fori_loop(0, i, body, stats)
    n_full = 2 * i
    stats = step(n_full, stats, 0, 0, False)
    stats = step(n_full + 1, stats, 1, tk, True)
    outs = []
    for hh in heads:
        acc = stats[hh][1] * acc_ref[hh] + pv(n_full + 1, hh, p_ref[hh, 0])
        outs.append(acc[:dv] / acc[dv:dv + 1])
    o_ref[0] = jnp.concatenate(outs, axis=0).T.astype(o_ref.dtype)


def _flash_attention(q_arr, k_arr, vt, scale, moba, q_blk0, k_blk0):
    b, s, _ = q_arr.shape
    dv = vt.shape[1] // N_HEADS
    qw = 2 * HEAD_DIM if moba else 2 * MLA_QK_PAD
    tq, tk = FLASH_TQ, FLASH_TK
    assert tq == 2 * tk and s % tq == 0 and (not moba or tk == MOBA_BLOCK)
    scratch = [pltpu.VMEM((2, 2, tk, tq), F32), pltpu.VMEM((2, 2, tk, tq), BF16),
               pltpu.VMEM((2, dv + ONES_ROWS, tq), F32)]
    if moba:
        scratch += [pltpu.VMEM((s // tk, qw), F32), pltpu.VMEM((2, s // tk, tq), F32)]
    return pl.pallas_call(
        functools.partial(_flash_kernel, tq=tq, tk=tk, c=scale * LOG2E, moba=moba),
        grid=(b, N_HEADS // 2, s // tq),
        in_specs=[pl.BlockSpec((1, tq, qw), lambda b_, h_, i: (b_, i, q_blk0 + h_)),
                  pl.BlockSpec((1, s, qw), lambda b_, h_, i: (b_, 0, k_blk0 + h_)),
                  pl.BlockSpec((1, 2 * dv, s), lambda b_, h_, i: (b_, h_, 0))],
        out_specs=pl.BlockSpec((1, tq, 2 * dv), lambda b_, h_, i: (b_, i, h_)),
        out_shape=jax.ShapeDtypeStruct((b, s, N_HEADS * dv), BF16),
        scratch_shapes=scratch,
        compiler_params=_cparams(("parallel", "parallel", "arbitrary")),
        name="moba_attention" if moba else "mla_attention",
    )(q_arr, k_arr, vt)


def _sb_kernel(q_ref, k_ref, v_ref, o_ref, acc_ref, stay_ref, *, t):
    i = pl.program_id(2)
    heads = range(2)
    r = lax.broadcasted_iota(jnp.int32, (t, t), 0)
    c = lax.broadcasted_iota(jnp.int32, (t, t), 1)
    later = jnp.where(r > c, 1.0, 0.0).astype(BF16)
    past = c < r
    qs = _split_heads(q_ref[0] * 0.125)

    def tile(j, diag):
        start = pl.multiple_of(j * t, t)
        k2 = k_ref[0, pl.ds(start, t), :]
        v2 = v_ref[0, pl.ds(start, t), :]
        top = None
        for hh in heads:
            z = _dot_nt(qs[hh], k2)
            lp = jnp.log1p(jnp.exp(-jnp.abs(z)))
            log_beta = jnp.minimum(z, 0.0) - lp
            log_stay = log_beta - z
            if diag:
                log_stay = jnp.where(past, log_stay, 0.0)
            hi = log_stay.astype(BF16)
            r1 = log_stay - hi.astype(F32)
            mid = r1.astype(BF16)
            lo = (r1 - mid.astype(F32)).astype(BF16)
            inner = _dot(hi, later) + _dot(mid, later) + _dot(lo, later)
            a = jnp.exp(log_beta + inner + stay_ref[hh])
            if diag:
                a = jnp.where(past, a, 0.0)
            acc_ref[hh] += _dot(a.astype(BF16), v2)
            stay = stay_ref[hh] + jnp.sum(log_stay, axis=1, keepdims=True)
            stay_ref[hh] = stay
            mx = jnp.max(stay)
            top = mx if top is None else jnp.maximum(top, mx)
        return top

    acc_ref[...] = jnp.zeros_like(acc_ref)
    stay_ref[...] = jnp.zeros_like(stay_ref)
    top = tile(i, True)

    def cond(state):
        j, top = state
        return jnp.logical_and(j >= 0, top > SB_UNDERFLOW)

    def body(state):
        j, _ = state
        return j - 1, tile(j, False)

    lax.while_loop(cond, body, (i - 1, top))
    low = lax.broadcasted_iota(jnp.int32, (t, 2 * HEAD_DIM), 1) < HEAD_DIM
    o_ref[0] = jnp.where(low, acc_ref[0], acc_ref[1]).astype(o_ref.dtype)


def _sb_attention(qkv, t=256):
    b, s, _ = qkv.shape
    w = 2 * HEAD_DIM
    npair = N_HEADS // 2
    return pl.pallas_call(
        functools.partial(_sb_kernel, t=t),
        grid=(b, npair, s // t),
        in_specs=[pl.BlockSpec((1, t, w), lambda b_, h_, i: (b_, i, h_)),
                  pl.BlockSpec((1, s, w), lambda b_, h_, i: (b_, 0, npair + h_)),
                  pl.BlockSpec((1, s, w), lambda b_, h_, i: (b_, 0, 2 * npair + h_))],
        out_specs=pl.BlockSpec((1, t, w), lambda b_, h_, i: (b_, i, h_)),
        out_shape=jax.ShapeDtypeStruct((b, s, N_HEADS * HEAD_DIM), BF16),
        scratch_shapes=[pltpu.VMEM((2, t, w), F32), pltpu.VMEM((2, t, 1), F32)],
        compiler_params=_cparams(("parallel", "parallel", "arbitrary")),
        name="sb_attention",
    )(qkv, qkv, qkv)


def _dil_proj_kernel(x_ref, g_ref, w_ref, c_ref, sp_ref, sm_ref, o_ref, *scratch, dil, nrope, half):
    tm = x_ref.shape[0]
    xn = _rms(x_ref[...], g_ref[...]).astype(BF16)
    y = _dot(xn, w_ref[...])
    reps = nrope // LANES
    cc = jnp.tile(c_ref[...], (1, reps))
    sp = jnp.tile(sp_ref[...], (1, reps))
    sm = jnp.tile(sm_ref[...], (1, reps))
    yr = y[:, :nrope]
    yr = yr * cc + pltpu.roll(yr, half, axis=1) * sp + pltpu.roll(yr, nrope - half, axis=1) * sm
    if dil == 1:
        o_ref[0, 0, :, :nrope] = yr.astype(o_ref.dtype)
        o_ref[0, 0, :, nrope:] = y[:, nrope:].astype(o_ref.dtype)
    else:
        y_ref, = scratch
        for cb in range(y.shape[1] // LANES):
            cols = slice(cb * LANES, (cb + 1) * LANES)
            y_ref[cb] = yr[:, cols] if cb < nrope // LANES else y[:, cols]
            for r in range(dil):
                o_ref[0, r, :, cols] = y_ref[cb, pl.ds(r, tm // dil, stride=dil), :].astype(o_ref.dtype)


def _dil_proj(x, g, w, rope, b, seq, dil, tm=512):
    m, d = x.shape
    n = w.shape[1]
    nblk = seq // tm
    nrope = 2 * N_HEADS * HEAD_DIM
    in_specs = [pl.BlockSpec((tm, d), lambda i: (i, 0)), pl.BlockSpec((1, d), lambda i: (0, 0)),
                pl.BlockSpec((d, n), lambda i: (0, 0))]
    in_specs += [pl.BlockSpec((tm, LANES), lambda i: (i % nblk, 0))] * 3
    return pl.pallas_call(
        functools.partial(_dil_proj_kernel, dil=dil, nrope=nrope, half=HEAD_DIM // 2),
        grid=(m // tm,), in_specs=in_specs,
        out_specs=pl.BlockSpec((1, dil, tm // dil, n), lambda i: (i // nblk, 0, i % nblk, 0)),
        out_shape=jax.ShapeDtypeStruct((b, dil, seq // dil, n), BF16),
        scratch_shapes=[] if dil == 1 else [pltpu.VMEM((n // LANES, tm, LANES), F32)],
        compiler_params=_cparams(("parallel",)),
        name="dil_proj",
    )(x, g.reshape(1, d), w, *rope)


def _dil_kernel(q_ref, k_ref, v_ref, o_ref, *, nblk, unroll):
    t = DIL_BLOCK
    r = lax.broadcasted_iota(jnp.int32, (t, t), 0)
    c = lax.broadcasted_iota(jnp.int32, (t, t), 1)
    low = lax.broadcasted_iota(jnp.int32, (t, 2 * HEAD_DIM), 1) < HEAD_DIM

    def block(n):
        start = pl.multiple_of(n * t, t)
        prev = pl.multiple_of(jnp.maximum(n - 1, 0) * t, t)
        first = jnp.where(n > 0, 0, t)
        qs = _split_heads(q_ref[0, 0, pl.ds(start, t), :] * 0.125)
        k_cur, k_prev = k_ref[0, 0, pl.ds(start, t), :], k_ref[0, 0, pl.ds(prev, t), :]
        v_cur, v_prev = v_ref[0, 0, pl.ds(start, t), :], v_ref[0, 0, pl.ds(prev, t), :]
        outs = []
        for hh in range(2):
            s_cur = jnp.where(c <= r, _dot_nt(qs[hh], k_cur), NEG_INF)
            s_prev = jnp.where(c >= r + first, _dot_nt(qs[hh], k_prev), NEG_INF)
            m = jnp.maximum(jnp.max(s_cur, axis=1, keepdims=True), jnp.max(s_prev, axis=1, keepdims=True))
            p_cur = jnp.exp(s_cur - m)
            p_prev = jnp.exp(s_prev - m)
            l = jnp.sum(p_cur, axis=1, keepdims=True) + jnp.sum(p_prev, axis=1, keepdims=True)
            acc = _dot(p_cur.astype(BF16), v_cur) + _dot(p_prev.astype(BF16), v_prev)
            lse = jnp.broadcast_to(m + jnp.log(l), acc.shape)
            outs.append(jnp.where(low, acc / l, lse) if hh == 0 else jnp.where(low, lse, acc / l))
        o_ref[0, 0, pl.ds(start, t), :] = jnp.concatenate(outs, axis=1)

    def body(nn, _):
        for u in range(unroll):
            block(nn * unroll + u)
        return 0

    lax.fori_loop(0, nblk // unroll, body, 0)


def _dil_attention(qkv):
    b, dil, length, _ = qkv.shape
    nblk = length // DIL_BLOCK
    w = 2 * HEAD_DIM
    npair = N_HEADS // 2

    def spec(first):
        return pl.BlockSpec((1, 1, length, w), lambda b_, r_, h_: (b_, r_, 0, first + h_))

    return pl.pallas_call(
        functools.partial(_dil_kernel, nblk=nblk, unroll=math.gcd(DIL_UNROLL, nblk)),
        grid=(b, dil, npair),
        in_specs=[spec(0), spec(npair), spec(2 * npair)],
        out_specs=pl.BlockSpec((1, 1, length, 2 * w), lambda b_, r_, h_: (b_, r_, 0, h_)),
        out_shape=jax.ShapeDtypeStruct((b, dil, length, N_HEADS * w), F32),
        compiler_params=_cparams(("parallel", "parallel", "parallel")),
        name="dil_attention",
    )(qkv, qkv, qkv)


def _dil_out_proj_kernel(a0_ref, a1_ref, a2_ref, w_ref, h_ref, o_ref, *scratch, dils):
    tm = h_ref.shape[0]
    xs = []
    scratch = list(scratch)
    for a_ref, dil in zip((a0_ref, a1_ref, a2_ref), dils):
        if dil == 1:
            xs.append(a_ref[0, 0])
        else:
            y_ref = scratch.pop(0)
            ncb = y_ref.shape[0]
            for cb in range(ncb):
                for r in range(dil):
                    y_ref[cb, pl.ds(r, tm // dil, stride=dil), :] = a_ref[0, r, :, cb * LANES:(cb + 1) * LANES]
            xs.append(jnp.concatenate([y_ref[cb] for cb in range(ncb)], axis=1))
    width = xs[0].shape[1]
    lane = lax.broadcasted_iota(jnp.int32, xs[0].shape, 1)
    low = lane % LANES < HEAD_DIM
    even = lane % (2 * LANES) < LANES
    mx = jnp.maximum(jnp.maximum(xs[0], xs[1]), xs[2])
    ws = [jnp.exp(x - mx) for x in xs]
    den = ws[0] + ws[1] + ws[2]
    comb = jnp.zeros_like(xs[0])
    for x, wgt in zip(xs, ws):
        wn = wgt / den
        wn = jnp.where(even, pltpu.roll(wn, width - HEAD_DIM, axis=1), pltpu.roll(wn, HEAD_DIM, axis=1))
        comb = comb + wn * x
    is_out = jnp.where(even, jnp.where(low, 1.0, 0.0), jnp.where(low, 0.0, 1.0))
    comb = jnp.where(is_out > 0.5, comb, 0.0).astype(BF16)
    o_ref[...] = h_ref[...] + _dot(comb, w_ref[...])


def _dil_out_proj(a_list, dils, w_exp, h, seq, tm=256):
    m, d = h.shape
    k = w_exp.shape[0]
    nblk = seq // tm
    in_specs = [pl.BlockSpec((1, dil, tm // dil, k), lambda i: (i // nblk, 0, i % nblk, 0)) for dil in dils]
    in_specs += [pl.BlockSpec((k, d), lambda i: (0, 0)), pl.BlockSpec((tm, d), lambda i: (i, 0))]
    return pl.pallas_call(
        functools.partial(_dil_out_proj_kernel, dils=dils), grid=(m // tm,),
        in_specs=in_specs,
        out_specs=pl.BlockSpec((tm, d), lambda i: (i, 0)),
        out_shape=jax.ShapeDtypeStruct((m, d), F32),
        scratch_shapes=[pltpu.VMEM((k // LANES, tm, LANES), F32) for dil in dils if dil > 1],
        compiler_params=_cparams(("parallel",)),
        name="dil_out_proj",
    )(*a_list, w_exp, h)


def _rope_tables(seq, dim, first_lane, period):
    half = dim // 2
    inv = ROPE_THETA ** (-jnp.arange(0, dim, 2, dtype=F32) / dim)
    ang = jnp.arange(seq, dtype=F32)[:, None] * inv[None, :]
    cos, sin = jnp.cos(ang), jnp.sin(ang)
    zeros = jnp.zeros_like(sin)
    pad_lo = jnp.zeros((seq, first_lane), F32)
    pad_hi = jnp.zeros((seq, period - first_lane - dim), F32)
    c = jnp.concatenate([pad_lo + 1.0, cos, cos, pad_hi + 1.0], axis=1)
    s_plus = jnp.concatenate([pad_lo, zeros, sin, pad_hi], axis=1)
    s_minus = jnp.concatenate([pad_lo, -sin, zeros, pad_hi], axis=1)
    reps = LANES // period
    return tuple(jnp.tile(t, (1, reps)) for t in (c, s_plus, s_minus))


def _mla_weights(w_in, w_uq, w_ukv):
    d = w_in.shape[0]
    ql, kvl = MLA_Q_LORA, MLA_KV_LORA
    w_kr = jnp.zeros((d, LANES), F32).at[:, MLA_NOPE:MLA_NOPE + MLA_ROPE].set(w_in[:, ql + kvl:])
    win = jnp.concatenate([w_in[:, :ql + kvl], w_kr], axis=1)
    wuq = w_uq.reshape(ql, N_HEADS, MLA_NOPE + MLA_ROPE)
    wuq = jnp.pad(wuq, ((0, 0), (0, 0), (0, MLA_QK_PAD - MLA_NOPE - MLA_ROPE))).reshape(ql, N_HEADS * MLA_QK_PAD)
    wukv = w_ukv.reshape(kvl, N_HEADS, MLA_NOPE + MLA_V)
    wuk = jnp.pad(wukv[:, :, :MLA_NOPE], ((0, 0), (0, 0), (0, MLA_QK_PAD - MLA_NOPE))).reshape(kvl, N_HEADS * MLA_QK_PAD)
    wuv = wukv[:, :, MLA_NOPE:].reshape(kvl, N_HEADS * MLA_V).T
    return win.astype(BF16), wuq.astype(BF16), wuk.astype(BF16), wuv.astype(BF16)


def kernel(x, norm_mix, norm_ffn, norm_final, a_w_qkv, a_w_o, b_w_in, b_q_norm, b_w_uq, b_kv_norm, b_w_ukv, b_w_o,
           c_w_qkv, c_w_o, d_w_qkv, d_w_o, ffn_w_gate, ffn_conv_w, ffn_conv_b, ffn_w_up, ffn_w_down):
    b, s, d = x.shape
    m = b * s
    hd = N_HEADS * HEAD_DIM
    depth = norm_mix.shape[0]
    rope_h = _rope_tables(s, HEAD_DIM, 0, HEAD_DIM)
    rope_r = _rope_tables(s, MLA_ROPE, MLA_NOPE, MLA_QK_PAD)
    h = x.reshape(m, d)
    for i in range(depth):
        kind, j = i % 4, i // 4
        g = norm_mix[i]
        if kind == 0:
            w = a_w_qkv[j].astype(BF16)
            qk = _norm_matmul(h, g, w[:, :2 * hd], s, rope=rope_h, half=HEAD_DIM // 2).reshape(b, s, 2 * hd)
            vt = _norm_matmul_t(h, g, w[:, 2 * hd:].T, b, s)
            o = _flash_attention(qk, qk, vt, HEAD_DIM ** -0.5, True, 0, N_HEADS // 2)
            h = _out_proj(o.reshape(m, hd), a_w_o[j].astype(BF16), h)
        elif kind == 1:
            win, wuq, wuk, wuv = _mla_weights(b_w_in[j], b_w_uq[j], b_w_ukv[j])
            q, k, vt = _mla_proj(h, g, win, b_q_norm[j], b_kv_norm[j], wuq, wuk, wuv, rope_r, b, s)
            o = _flash_attention(q.reshape(b, s, -1), k.reshape(b, s, -1), vt, (MLA_NOPE + MLA_ROPE) ** -0.5,
                                 False, 0, 0)
            h = _out_proj(o.reshape(m, hd), b_w_o[j].astype(BF16), h)
        elif kind == 2:
            w = c_w_qkv[j].astype(BF16)
            dils = tuple(dil for _, dil in DIL_GROUPS)
            outs = [_dil_attention(_dil_proj(h, g, w[:, gi * 3 * hd:(gi + 1) * 3 * hd], rope_h, b, s, dil))
                    for gi, dil in enumerate(dils)]
            w_o = c_w_o[j].astype(BF16).reshape(N_HEADS // 2, 2, HEAD_DIM, d)
            zero = jnp.zeros_like(w_o[:, 0])
            w_o = jnp.concatenate([w_o[:, 0], zero, zero, w_o[:, 1]], axis=1).reshape(N_HEADS * 2 * HEAD_DIM, d)
            h = _dil_out_proj(outs, dils, w_o, h, s)
        else:
            qkv = _norm_matmul(h, g, d_w_qkv[j].astype(BF16), s).reshape(b, s, 3 * hd)
            h = _out_proj(_sb_attention(qkv).reshape(m, hd), d_w_o[j].astype(BF16), h)
        h = _ffn(h, norm_ffn[i], ffn_w_gate[i].astype(BF16), ffn_w_up[i].astype(BF16), ffn_w_down[i].astype(BF16),
                 ffn_conv_w[i], ffn_conv_b[i], s)
    return _final_norm(h, norm_final).reshape(b, s, d)
```

```python
import functools
import math

import jax
import jax.numpy as jnp
from jax import lax
from jax.experimental import pallas as pl
from jax.experimental.pallas import tpu as pltpu

D_MODEL = 1024
HEAD_DIM = 64
N_HEADS = 16
ROPE_THETA = 10000.0
RMS_EPS = 1e-6
NEG_INF = -1e30
MOBA_BLOCK = 256
MOBA_TOPK = 3
MLA_Q_LORA = 384
MLA_KV_LORA = 256
MLA_NOPE = 64
MLA_ROPE = 32
MLA_V = 64
MLA_QK_PAD = 128
DIL_GROUPS = ((128, 1), (512, 4), (2048, 16))
DIL_BLOCK = 128
D_FF = 2816
CONV_WIDTH = 3
LANES = 128
SB_UNDERFLOW = -104.0
VMEM_LIMIT = 56 * 1024 * 1024
LOG2E = 1.4426950408889634
MLA_Q_SCALE = (MLA_NOPE + MLA_ROPE) ** -0.5 * LOG2E
MOBA_Q_SCALE = HEAD_DIM ** -0.5 * LOG2E
FLASH_TQ = 512
FLASH_TK = 256
ONES_ROWS = 16
SB_PAIRS = 2
DIL_UNROLL = 4

F32 = jnp.float32
BF16 = jnp.bfloat16


def _cparams(sem):
    return pltpu.CompilerParams(dimension_semantics=sem, vmem_limit_bytes=VMEM_LIMIT)


def _dot(a, b):
    return jnp.dot(a, b, preferred_element_type=F32)


def _dot_nt(a, b):
    return lax.dot_general(a, b, (((1,), (1,)), ((), ())), preferred_element_type=F32)


def _rms(x, g):
    return x * lax.rsqrt(jnp.mean(x * x, axis=-1, keepdims=True) + RMS_EPS) * g


def _norm_matmul_kernel(*refs, rope, half, tn):
    if rope:
        x_ref, g_ref, w_ref, c_ref, sp_ref, sm_ref, scale_ref, o_ref = refs
    else:
        x_ref, g_ref, w_ref, o_ref = refs
    xn = _rms(x_ref[...], g_ref[...]).astype(BF16)
    y = _dot(xn, w_ref[...])
    if rope:
        reps = tn // LANES
        c = jnp.tile(c_ref[...], (1, reps))
        sp = jnp.tile(sp_ref[...], (1, reps))
        sm = jnp.tile(sm_ref[...], (1, reps))
        y = y * c + pltpu.roll(y, half, axis=1) * sp + pltpu.roll(y, tn - half, axis=1) * sm
        y = y * scale_ref[...]
    o_ref[...] = y.astype(o_ref.dtype)


def _norm_matmul(x, g, w, seq, *, rope=None, half=0, colscale=None, out_dtype=BF16, tm=512, tn=None):
    m, d = x.shape
    n = w.shape[1]
    tn = n if tn is None else tn
    grid = (n // tn, m // tm)
    in_specs = [pl.BlockSpec((tm, d), lambda j, i: (i, 0)),
                pl.BlockSpec((1, d), lambda j, i: (0, 0)),
                pl.BlockSpec((d, tn), lambda j, i: (0, j))]
    args = [x, g.reshape(1, d), w]
    if rope is not None:
        nblk = seq // tm
        for t in rope:
            in_specs.append(pl.BlockSpec((tm, LANES), lambda j, i: (i % nblk, 0)))
            args.append(t)
        in_specs.append(pl.BlockSpec((1, tn), lambda j, i: (0, j)))
        args.append(colscale.reshape(1, n))
    return pl.pallas_call(
        functools.partial(_norm_matmul_kernel, rope=rope is not None, half=half, tn=tn),
        grid=grid, in_specs=in_specs,
        out_specs=pl.BlockSpec((tm, tn), lambda j, i: (i, j)),
        out_shape=jax.ShapeDtypeStruct((m, n), out_dtype),
        compiler_params=_cparams(("parallel", "parallel")),
        name="norm_matmul",
    )(*args)


def _norm_matmul_t_kernel(x_ref, g_ref, wt_ref, o_ref):
    xn = _rms(x_ref[...], g_ref[...]).astype(BF16)
    o_ref[0] = _dot_nt(wt_ref[...], xn).astype(o_ref.dtype)


def _norm_matmul_t(x, g, wt, b, seq, tm=512):
    m, d = x.shape
    n = wt.shape[0]
    nblk = seq // tm
    return pl.pallas_call(
        _norm_matmul_t_kernel, grid=(m // tm,),
        in_specs=[pl.BlockSpec((tm, d), lambda i: (i, 0)), pl.BlockSpec((1, d), lambda i: (0, 0)),
                  pl.BlockSpec((n, d), lambda i: (0, 0))],
        out_specs=pl.BlockSpec((1, n, tm), lambda i: (i // nblk, 0, i % nblk)),
        out_shape=jax.ShapeDtypeStruct((b, n, seq), BF16),
        compiler_params=_cparams(("parallel",)),
        name="norm_matmul_t",
    )(x, g.reshape(1, d), wt)


def _mla_proj_kernel(x_ref, g_ref, win_ref, qn_ref, kvn_ref, wuq_ref, wuk_ref, wuv_ref,
                     c_ref, sp_ref, sm_ref, q_ref, k_ref, v_ref):
    half = MLA_ROPE // 2
    xn = _rms(x_ref[...], g_ref[...]).astype(BF16)
    c = _dot(xn, win_ref[...])
    cq = _rms(c[:, :MLA_Q_LORA], qn_ref[...]).astype(BF16)
    ckv = _rms(c[:, MLA_Q_LORA:MLA_Q_LORA + MLA_KV_LORA], kvn_ref[...]).astype(BF16)
    kr = c[:, MLA_Q_LORA + MLA_KV_LORA:]

    def rot(y, width):
        reps = width // LANES
        cc = jnp.tile(c_ref[...], (1, reps))
        sp = jnp.tile(sp_ref[...], (1, reps))
        sm = jnp.tile(sm_ref[...], (1, reps))
        return y * cc + pltpu.roll(y, half, axis=1) * sp + pltpu.roll(y, width - half, axis=1) * sm

    nq = wuq_ref.shape[1]
    q = rot(_dot(cq, wuq_ref[...]), nq) * MLA_Q_SCALE
    q_ref[...] = q.astype(q_ref.dtype)
    kr = rot(kr, LANES)
    k = _dot(ckv, wuk_ref[...]) + jnp.tile(kr, (1, nq // LANES))
    k_ref[...] = k.astype(k_ref.dtype)
    v_ref[0] = _dot_nt(wuv_ref[...], ckv).astype(v_ref.dtype)


def _mla_proj(x, g, win, qn, kvn, wuq, wuk, wuv, rope, b, seq, tm=512):
    m, d = x.shape
    nblk = seq // tm
    full = lambda a: pl.BlockSpec(a.shape, lambda i: (0, 0))
    qn2, kvn2, g2 = qn.reshape(1, -1), kvn.reshape(1, -1), g.reshape(1, -1)
    in_specs = [pl.BlockSpec((tm, d), lambda i: (i, 0)), full(g2), full(win), full(qn2), full(kvn2),
                full(wuq), full(wuk), full(wuv)]
    in_specs += [pl.BlockSpec((tm, LANES), lambda i: (i % nblk, 0))] * 3
    nq, nv = wuq.shape[1], wuv.shape[0]
    return pl.pallas_call(
        _mla_proj_kernel, grid=(m // tm,), in_specs=in_specs,
        out_specs=[pl.BlockSpec((tm, nq), lambda i: (i, 0)), pl.BlockSpec((tm, nq), lambda i: (i, 0)),
                   pl.BlockSpec((1, nv, tm), lambda i: (i // nblk, 0, i % nblk))],
        out_shape=[jax.ShapeDtypeStruct((m, nq), BF16), jax.ShapeDtypeStruct((m, nq), BF16),
                   jax.ShapeDtypeStruct((b, nv, seq), BF16)],
        compiler_params=_cparams(("parallel",)),
        name="mla_proj",
    )(x, g2, win, qn2, kvn2, wuq, wuk, wuv, *rope)


def _out_proj_kernel(a_ref, w_ref, h_ref, o_ref):
    o_ref[...] = h_ref[...] + _dot(a_ref[...], w_ref[...])


def _out_proj(a, w, h, tm=512):
    m, k = a.shape
    d = w.shape[1]
    return pl.pallas_call(
        _out_proj_kernel, grid=(m // tm,),
        in_specs=[pl.BlockSpec((tm, k), lambda i: (i, 0)), pl.BlockSpec((k, d), lambda i: (0, 0)),
                  pl.BlockSpec((tm, d), lambda i: (i, 0))],
        out_specs=pl.BlockSpec((tm, d), lambda i: (i, 0)),
        out_shape=jax.ShapeDtypeStruct((m, d), F32),
        compiler_params=_cparams(("parallel",)),
        name="out_proj",
    )(a, w, h)


def _gelu_exact(x):
    return 0.5 * x * (1.0 + lax.erf(x * (1.0 / math.sqrt(2.0))))


def _ffn_kernel(x_ref, g_ref, wg_ref, wu_ref, wd_ref, cw_ref, cb_ref, o_ref, xn_ref, acc_ref, carry_ref,
                *, nblk, fc, final_norm):
    i, c = pl.program_id(0), pl.program_id(1)
    nc = pl.num_programs(1)
    tm = x_ref.shape[0]

    @pl.when(c == 0)
    def _():
        xn_ref[...] = _rms(x_ref[...], g_ref[...]).astype(BF16)
        acc_ref[...] = jnp.zeros_like(acc_ref)

    @pl.when(jnp.logical_and(c == 0, i % nblk == 0))
    def _():
        carry_ref[...] = jnp.zeros_like(carry_ref)

    xn = xn_ref[...]
    gate = _dot(xn, wg_ref[...])
    up = _dot(xn, wu_ref[...])
    col = pl.multiple_of(c * fc, LANES)
    prev = carry_ref[:, pl.ds(col, fc)]
    row = lax.broadcasted_iota(jnp.int32, gate.shape, 0)
    p1, p2 = prev[7:8, :], prev[6:7, :]
    s1 = jnp.where(row == 0, p1, pltpu.roll(gate, 1, axis=0))
    s2 = jnp.where(row == 0, p2, jnp.where(row == 1, p1, pltpu.roll(gate, 2, axis=0)))
    carry_ref[:, pl.ds(col, fc)] = gate[tm - 8:, :]
    cw = cw_ref[0]
    gc = cb_ref[0] + cw[0:1, :] * s2 + cw[1:2, :] * s1 + cw[2:3, :] * gate
    hidden = (_gelu_exact(gc) * up).astype(BF16)
    acc_ref[...] += _dot(hidden, wd_ref[...])

    @pl.when(c == nc - 1)
    def _():
        o_ref[...] = x_ref[...] + acc_ref[...]


def _ffn(h, g, wg, wu, wd, cw, cb, seq, tm=512, fc=1408):
    m, d = h.shape
    f = wg.shape[1]
    nc = f // fc
    cw3 = cw.reshape(CONV_WIDTH, nc, fc).transpose(1, 0, 2)
    cb3 = cb.reshape(nc, 1, fc)
    return pl.pallas_call(
        functools.partial(_ffn_kernel, nblk=seq // tm, fc=fc, final_norm=False),
        grid=(m // tm, nc),
        in_specs=[pl.BlockSpec((tm, d), lambda i, c: (i, 0)),
                  pl.BlockSpec((1, d), lambda i, c: (0, 0)),
                  pl.BlockSpec((d, fc), lambda i, c: (0, c)),
                  pl.BlockSpec((d, fc), lambda i, c: (0, c)),
                  pl.BlockSpec((fc, d), lambda i, c: (c, 0)),
                  pl.BlockSpec((1, CONV_WIDTH, fc), lambda i, c: (c, 0, 0)),
                  pl.BlockSpec((1, 1, fc), lambda i, c: (c, 0, 0))],
        out_specs=pl.BlockSpec((tm, d), lambda i, c: (i, 0)),
        out_shape=jax.ShapeDtypeStruct((m, d), F32),
        scratch_shapes=[pltpu.VMEM((tm, d), BF16), pltpu.VMEM((tm, d), F32), pltpu.VMEM((8, f), F32)],
        compiler_params=_cparams(("arbitrary", "arbitrary")),
        name="conv_ffn",
    )(h, g.reshape(1, d), wg, wu, wd, cw3, cb3)


def _final_norm_kernel(x_ref, g_ref, o_ref):
    o_ref[...] = _rms(x_ref[...], g_ref[...])


def _final_norm(h, g, tm=1024):
    m, d = h.shape
    return pl.pallas_call(
        _final_norm_kernel, grid=(m // tm,),
        in_specs=[pl.BlockSpec((tm, d), lambda i: (i, 0)), pl.BlockSpec((1, d), lambda i: (0, 0))],
        out_specs=pl.BlockSpec((tm, d), lambda i: (i, 0)),
        out_shape=jax.ShapeDtypeStruct((m, d), F32),
        compiler_params=_cparams(("parallel",)),
        name="final_norm",
    )(h, g.reshape(1, d))


def _split_heads(x2):
    x = x2.astype(F32)
    low = lax.broadcasted_iota(jnp.int32, x.shape, 1) < HEAD_DIM
    return [jnp.where(low, x, 0.0).astype(BF16), jnp.where(low, 0.0, x).astype(BF16)]


def _flash_kernel(q_ref, k_ref, vt_ref, o_ref, *scratch, tq, tk, moba):
    i = pl.program_id(2)
    heads = range(2)
    s_ref = (scratch[0:2], scratch[2:4])
    p_ref = (scratch[4:6], scratch[6:8])
    acc_ref = scratch[8:10]
    st_ref = scratch[10]
    moba_refs = scratch[11:]
    dv = vt_ref.shape[1] // 2
    ones = jnp.ones((ONES_ROWS, tk), BF16)
    krow = lax.broadcasted_iota(jnp.int32, (tk, tq), 0)
    qcol = lax.broadcasted_iota(jnp.int32, (tk, tq), 1)

    if moba:
        kmean_ref, sel_ref = moba_refs
        nb = kmean_ref.shape[0]
        qs = _split_heads(q_ref[0])

        def ktile(start, hh):
            return k_ref[0, pl.ds(start, tk), :]

        @pl.when(i == 0)
        def _():
            for n in range(nb):
                rows = k_ref[0, n * tk:(n + 1) * tk, :].astype(F32)
                kmean_ref[n:n + 1, :] = jnp.sum(rows, axis=0, keepdims=True) * (1.0 / tk)

        km = kmean_ref[...]
        km_hi = km.astype(BF16)
        km_lo = (km - km_hi.astype(F32)).astype(BF16)
        blk = lax.broadcasted_iota(jnp.int32, (nb, tq), 0)
        lane = lax.broadcasted_iota(jnp.int32, (nb, tq), 1)
        qblk = i * (tq // tk)
        for t in range(1, tq // tk):
            qblk = qblk + jnp.where(lane >= t * tk, 1, 0)
        past = blk < qblk
        blk_f = blk.astype(F32)
        for hh in heads:
            gate = _dot_nt(km_hi, qs[hh]) + _dot_nt(km_lo, qs[hh])
            g = jnp.where(past, gate, NEG_INF)
            picked = jnp.zeros(gate.shape, F32)
            for _ in range(MOBA_TOPK):
                mx = jnp.max(g, axis=0, keepdims=True)
                first = jnp.min(jnp.where(g == mx, blk_f, float(nb)), axis=0, keepdims=True)
                hit = blk_f == first
                picked = jnp.where(hit, 1.0, picked)
                g = jnp.where(hit, -jnp.inf, g)
            sel_ref[hh] = jnp.where(past, picked, 0.0)
    else:
        dk = q_ref.shape[2] // 2
        qs = [q_ref[0, :, hh * dk:(hh + 1) * dk] for hh in heads]

        def ktile(start, hh):
            return k_ref[0, pl.ds(start, tk), hh * dk:(hh + 1) * dk]

    def mask(j, hh, s, off):
        if off is None:
            if not moba:
                return s
            return jnp.where(sel_ref[hh, pl.ds(j, 1), :] > 0.5, s, NEG_INF)
        causal = krow + off <= qcol
        if not moba:
            return jnp.where(causal, s, NEG_INF)
        same = jnp.where(jnp.logical_and(qcol >= off, qcol < off + tk), 1.0, 0.0)
        later = jnp.where(qcol >= off + tk, sel_ref[hh, pl.ds(j, 1), :], 0.0)
        allowed = jnp.where(same > 0.5, jnp.where(causal, 1.0, 0.0), later)
        return jnp.where(allowed > 0.5, s, NEG_INF)

    def qk(j, hh):
        start = pl.multiple_of(j * tk, tk)
        return _dot_nt(ktile(start, hh), qs[hh])

    def pv(j, hh, p):
        start = pl.multiple_of(jnp.maximum(j, 0) * tk, tk)
        vt = jnp.concatenate([vt_ref[0, hh * dv:(hh + 1) * dv, pl.ds(start, tk)], ones], axis=0)
        return _dot(vt, p)

    def step(j, cur, off, last):
        if not last:
            for hh in heads:
                s_ref[hh][1 - cur][...] = qk(j + 1, hh)
        pvrs = [pv(j - 1, hh, p_ref[hh][cur][...]) for hh in heads]
        for hh in heads:
            m, alpha_prev = st_ref[2 * hh:2 * hh + 1, :], st_ref[2 * hh + 1:2 * hh + 2, :]
            s = mask(j, hh, s_ref[hh][cur][...], off)
            m_new = jnp.maximum(m, jnp.max(s, axis=0, keepdims=True))
            alpha = jnp.exp2(m - m_new)
            p = jnp.exp2(s - m_new)
            acc_ref[hh][...] = alpha_prev * acc_ref[hh][...] + pvrs[hh]
            p_ref[hh][1 - cur][...] = p.astype(BF16)
            st_ref[2 * hh:2 * hh + 1, :] = m_new
            st_ref[2 * hh + 1:2 * hh + 2, :] = alpha

    for hh in heads:
        s_ref[hh][0][...] = qk(0, hh)
        p_ref[hh][0][...] = jnp.zeros((tk, tq), BF16)
        acc_ref[hh][...] = jnp.zeros_like(acc_ref[hh])
        st_ref[2 * hh:2 * hh + 1, :] = jnp.full((1, tq), NEG_INF, F32)
        st_ref[2 * hh + 1:2 * hh + 2, :] = jnp.ones((1, tq), F32)

    def body(jj, carry):
        step(2 * jj, 0, None, False)
        step(2 * jj + 1, 1, None, False)
        return carry

    lax.fori_loop(0, i, body, 0)
    n_full = 2 * i
    step(n_full, 0, 0, False)
    step(n_full + 1, 1, tk, True)
    outs = []
    for hh in heads:
        acc = st_ref[2 * hh + 1:2 * hh + 2, :] * acc_ref[hh][...] + pv(n_full + 1, hh, p_ref[hh][0][...])
        outs.append(acc[:dv] / acc[dv:dv + 1])
    o_ref[0] = jnp.concatenate(outs, axis=0).T.astype(o_ref.dtype)


def _flash_attention(q_arr, k_arr, vt, moba, q_blk0, k_blk0):
    b, s, _ = q_arr.shape
    dv = vt.shape[1] // N_HEADS
    qw = 2 * HEAD_DIM if moba else 2 * MLA_QK_PAD
    tq, tk = FLASH_TQ, FLASH_TK
    assert tq == 2 * tk and s % tq == 0 and (not moba or tk == MOBA_BLOCK)
    scratch = ([pltpu.VMEM((tk, tq), F32)] * 4 + [pltpu.VMEM((tk, tq), BF16)] * 4
               + [pltpu.VMEM((dv + ONES_ROWS, tq), F32)] * 2 + [pltpu.VMEM((8, tq), F32)])
    if moba:
        scratch += [pltpu.VMEM((s // tk, qw), F32), pltpu.VMEM((2, s // tk, tq), F32)]
    return pl.pallas_call(
        functools.partial(_flash_kernel, tq=tq, tk=tk, moba=moba),
        grid=(b, N_HEADS // 2, s // tq),
        in_specs=[pl.BlockSpec((1, tq, qw), lambda b_, h_, i: (b_, i, q_blk0 + h_)),
                  pl.BlockSpec((1, s, qw), lambda b_, h_, i: (b_, 0, k_blk0 + h_)),
                  pl.BlockSpec((1, 2 * dv, s), lambda b_, h_, i: (b_, h_, 0))],
        out_specs=pl.BlockSpec((1, tq, 2 * dv), lambda b_, h_, i: (b_, i, h_)),
        out_shape=jax.ShapeDtypeStruct((b, s, N_HEADS * dv), BF16),
        scratch_shapes=scratch,
        compiler_params=_cparams(("parallel", "parallel", "arbitrary")),
        name="moba_attention" if moba else "mla_attention",
    )(q_arr, k_arr, vt)


def _sb_kernel(q_ref, k_ref, v_ref, o_ref, acc_ref, stay_ref, *, t):
    i = pl.program_id(2)
    w = 2 * HEAD_DIM
    heads = range(2 * SB_PAIRS)
    r = lax.broadcasted_iota(jnp.int32, (t, t), 0)
    c = lax.broadcasted_iota(jnp.int32, (t, t), 1)
    later = jnp.where(r > c, 1.0, 0.0).astype(BF16)
    past = c < r
    qs = []
    for pp in range(SB_PAIRS):
        qs += _split_heads(q_ref[0, :, pp * w:(pp + 1) * w] * 0.125)

    def tile(j, diag):
        start = pl.multiple_of(j * t, t)
        cols = [slice((hh // 2) * w, (hh // 2 + 1) * w) for hh in heads]
        zs = [_dot_nt(qs[hh], k_ref[0, pl.ds(start, t), cols[hh]]) for hh in heads]
        parts = []
        for z in zs:
            lp = jnp.log(1.0 + jnp.exp(-jnp.abs(z)))
            log_beta = jnp.minimum(z, 0.0) - lp
            log_stay = log_beta - z
            if diag:
                log_stay = jnp.where(past, log_stay, 0.0)
            hi = log_stay.astype(BF16)
            lo = (log_stay - hi.astype(F32)).astype(BF16)
            parts.append((log_beta, log_stay, hi, lo))
        inners = [_dot(hi, later) + _dot(lo, later) for _, _, hi, lo in parts]
        weights = []
        top = None
        for hh in heads:
            log_beta, log_stay, _, _ = parts[hh]
            a = jnp.exp(log_beta + inners[hh] + stay_ref[hh])
            if diag:
                a = jnp.where(past, a, 0.0)
            weights.append(a.astype(BF16))
            stay = stay_ref[hh] + jnp.sum(log_stay, axis=1, keepdims=True)
            stay_ref[hh] = stay
            mx = jnp.max(stay)
            top = mx if top is None else jnp.maximum(top, mx)
        for hh in heads:
            acc_ref[hh] += _dot(weights[hh], v_ref[0, pl.ds(start, t), cols[hh]])
        return top

    acc_ref[...] = jnp.zeros_like(acc_ref)
    stay_ref[...] = jnp.zeros_like(stay_ref)
    top = tile(i, True)

    def cond(state):
        j, top = state
        return jnp.logical_and(j >= 0, top > SB_UNDERFLOW)

    def body(state):
        j, _ = state
        return j - 1, tile(j, False)

    lax.while_loop(cond, body, (i - 1, top))
    low = lax.broadcasted_iota(jnp.int32, (t, w), 1) < HEAD_DIM
    for pp in range(SB_PAIRS):
        o_ref[0, :, pp * w:(pp + 1) * w] = jnp.where(low, acc_ref[2 * pp], acc_ref[2 * pp + 1]).astype(o_ref.dtype)


def _sb_attention(qkv, t=256):
    b, s, _ = qkv.shape
    w = 2 * HEAD_DIM * SB_PAIRS
    npair = N_HEADS // (2 * SB_PAIRS)
    return pl.pallas_call(
        functools.partial(_sb_kernel, t=t),
        grid=(b, npair, s // t),
        in_specs=[pl.BlockSpec((1, t, w), lambda b_, h_, i: (b_, i, h_)),
                  pl.BlockSpec((1, s, w), lambda b_, h_, i: (b_, 0, npair + h_)),
                  pl.BlockSpec((1, s, w), lambda b_, h_, i: (b_, 0, 2 * npair + h_))],
        out_specs=pl.BlockSpec((1, t, w), lambda b_, h_, i: (b_, i, h_)),
        out_shape=jax.ShapeDtypeStruct((b, s, N_HEADS * HEAD_DIM), BF16),
        scratch_shapes=[pltpu.VMEM((2 * SB_PAIRS, t, 2 * HEAD_DIM), F32), pltpu.VMEM((2 * SB_PAIRS, t, 1), F32)],
        compiler_params=_cparams(("parallel", "parallel", "arbitrary")),
        name="sb_attention",
    )(qkv, qkv, qkv)


def _dil_proj_kernel(x_ref, g_ref, w_ref, c_ref, sp_ref, sm_ref, o_ref, *scratch, dil, nrope, half):
    tm = x_ref.shape[0]
    xn = _rms(x_ref[...], g_ref[...]).astype(BF16)
    y = _dot(xn, w_ref[...])
    reps = nrope // LANES
    cc = jnp.tile(c_ref[...], (1, reps))
    sp = jnp.tile(sp_ref[...], (1, reps))
    sm = jnp.tile(sm_ref[...], (1, reps))
    yr = y[:, :nrope]
    yr = yr * cc + pltpu.roll(yr, half, axis=1) * sp + pltpu.roll(yr, nrope - half, axis=1) * sm
    if dil == 1:
        o_ref[0, 0, :, :nrope] = yr.astype(o_ref.dtype)
        o_ref[0, 0, :, nrope:] = y[:, nrope:].astype(o_ref.dtype)
    else:
        y_ref, = scratch
        for cb in range(y.shape[1] // LANES):
            cols = slice(cb * LANES, (cb + 1) * LANES)
            y_ref[cb] = yr[:, cols] if cb < nrope // LANES else y[:, cols]
            for r in range(dil):
                o_ref[0, r, :, cols] = y_ref[cb, pl.ds(r, tm // dil, stride=dil), :].astype(o_ref.dtype)


def _dil_proj(x, g, w, rope, b, seq, dil, tm=512):
    m, d = x.shape
    n = w.shape[1]
    nblk = seq // tm
    nrope = 2 * N_HEADS * HEAD_DIM
    in_specs = [pl.BlockSpec((tm, d), lambda i: (i, 0)), pl.BlockSpec((1, d), lambda i: (0, 0)),
                pl.BlockSpec((d, n), lambda i: (0, 0))]
    in_specs += [pl.BlockSpec((tm, LANES), lambda i: (i % nblk, 0))] * 3
    return pl.pallas_call(
        functools.partial(_dil_proj_kernel, dil=dil, nrope=nrope, half=HEAD_DIM // 2),
        grid=(m // tm,), in_specs=in_specs,
        out_specs=pl.BlockSpec((1, dil, tm // dil, n), lambda i: (i // nblk, 0, i % nblk, 0)),
        out_shape=jax.ShapeDtypeStruct((b, dil, seq // dil, n), BF16),
        scratch_shapes=[] if dil == 1 else [pltpu.VMEM((n // LANES, tm, LANES), F32)],
        compiler_params=_cparams(("parallel",)),
        name="dil_proj",
    )(x, g.reshape(1, d), w, *rope)


def _dil_kernel(q_ref, k_ref, v_ref, o_ref, *, nblk, unroll):
    t = DIL_BLOCK
    r = lax.broadcasted_iota(jnp.int32, (t, t), 0)
    c = lax.broadcasted_iota(jnp.int32, (t, t), 1)
    low = lax.broadcasted_iota(jnp.int32, (t, 2 * HEAD_DIM), 1) < HEAD_DIM

    def body(nn, _):
        scores, values = [], []
        for u in range(unroll):
            n = nn * unroll + u
            start = pl.multiple_of(n * t, t)
            prev = pl.multiple_of(jnp.maximum(n - 1, 0) * t, t)
            first = jnp.where(n > 0, 0, t)
            qs = _split_heads(q_ref[0, 0, pl.ds(start, t), :] * 0.125)
            k_cur, k_prev = k_ref[0, 0, pl.ds(start, t), :], k_ref[0, 0, pl.ds(prev, t), :]
            values.append((start, v_ref[0, 0, pl.ds(start, t), :], v_ref[0, 0, pl.ds(prev, t), :]))
            for hh in range(2):
                s_cur = jnp.where(c <= r, _dot_nt(qs[hh], k_cur), NEG_INF)
                s_prev = jnp.where(c >= r + first, _dot_nt(qs[hh], k_prev), NEG_INF)
                scores.append((s_cur, s_prev))
        probs = []
        for s_cur, s_prev in scores:
            m = jnp.max(jnp.maximum(s_cur, s_prev), axis=1, keepdims=True)
            p_cur = jnp.exp(s_cur - m)
            p_prev = jnp.exp(s_prev - m)
            l = jnp.sum(p_cur + p_prev, axis=1, keepdims=True)
            probs.append((p_cur.astype(BF16), p_prev.astype(BF16), l, m + jnp.log(l)))
        for u, (start, v_cur, v_prev) in enumerate(values):
            outs = []
            for hh in range(2):
                p_cur, p_prev, l, lse = probs[2 * u + hh]
                acc = _dot(p_cur, v_cur) + _dot(p_prev, v_prev)
                lse = jnp.broadcast_to(lse, acc.shape)
                outs.append(jnp.where(low, acc / l, lse) if hh == 0 else jnp.where(low, lse, acc / l))
            o_ref[0, 0, pl.ds(start, t), :] = jnp.concatenate(outs, axis=1)
        return 0

    lax.fori_loop(0, nblk // unroll, body, 0)


def _dil_attention(qkv):
    b, dil, length, _ = qkv.shape
    nblk = length // DIL_BLOCK
    w = 2 * HEAD_DIM
    npair = N_HEADS // 2

    def spec(first):
        return pl.BlockSpec((1, 1, length, w), lambda b_, r_, h_: (b_, r_, 0, first + h_))

    return pl.pallas_call(
        functools.partial(_dil_kernel, nblk=nblk, unroll=math.gcd(DIL_UNROLL, nblk)),
        grid=(b, dil, npair),
        in_specs=[spec(0), spec(npair), spec(2 * npair)],
        out_specs=pl.BlockSpec((1, 1, length, 2 * w), lambda b_, r_, h_: (b_, r_, 0, h_)),
        out_shape=jax.ShapeDtypeStruct((b, dil, length, N_HEADS * w), F32),
        compiler_params=_cparams(("parallel", "parallel", "parallel")),
        name="dil_attention",
    )(qkv, qkv, qkv)


def _dil_out_proj_kernel(a0_ref, a1_ref, a2_ref, w_ref, h_ref, o_ref, *scratch, dils):
    tm = h_ref.shape[0]
    xs = []
    scratch = list(scratch)
    for a_ref, dil in zip((a0_ref, a1_ref, a2_ref), dils):
        if dil == 1:
            xs.append(a_ref[0, 0])
        else:
            y_ref = scratch.pop(0)
            ncb = y_ref.shape[0]
            for cb in range(ncb):
                for r in range(dil):
                    y_ref[cb, pl.ds(r, tm // dil, stride=dil), :] = a_ref[0, r, :, cb * LANES:(cb + 1) * LANES]
            xs.append(jnp.concatenate([y_ref[cb] for cb in range(ncb)], axis=1))
    width = xs[0].shape[1]
    lane = lax.broadcasted_iota(jnp.int32, xs[0].shape, 1)
    low = lane % LANES < HEAD_DIM
    even = lane % (2 * LANES) < LANES
    mx = jnp.maximum(jnp.maximum(xs[0], xs[1]), xs[2])
    ws = [jnp.exp(x - mx) for x in xs]
    den = ws[0] + ws[1] + ws[2]
    comb = jnp.zeros_like(xs[0])
    for x, wgt in zip(xs, ws):
        wn = wgt / den
        wn = jnp.where(even, pltpu.roll(wn, width - HEAD_DIM, axis=1), pltpu.roll(wn, HEAD_DIM, axis=1))
        comb = comb + wn * x
    is_out = jnp.where(even, jnp.where(low, 1.0, 0.0), jnp.where(low, 0.0, 1.0))
    comb = jnp.where(is_out > 0.5, comb, 0.0).astype(BF16)
    o_ref[...] = h_ref[...] + _dot(comb, w_ref[...])


def _dil_out_proj(a_list, dils, w_exp, h, seq, tm=256):
    m, d = h.shape
    k = w_exp.shape[0]
    nblk = seq // tm
    in_specs = [pl.BlockSpec((1, dil, tm // dil, k), lambda i: (i // nblk, 0, i % nblk, 0)) for dil in dils]
    in_specs += [pl.BlockSpec((k, d), lambda i: (0, 0)), pl.BlockSpec((tm, d), lambda i: (i, 0))]
    return pl.pallas_call(
        functools.partial(_dil_out_proj_kernel, dils=dils), grid=(m // tm,),
        in_specs=in_specs,
        out_specs=pl.BlockSpec((tm, d), lambda i: (i, 0)),
        out_shape=jax.ShapeDtypeStruct((m, d), F32),
        scratch_shapes=[pltpu.VMEM((k // LANES, tm, LANES), F32) for dil in dils if dil > 1],
        compiler_params=_cparams(("parallel",)),
        name="dil_out_proj",
    )(*a_list, w_exp, h)


def _rope_tables(seq, dim, first_lane, period):
    half = dim // 2
    inv = ROPE_THETA ** (-jnp.arange(0, dim, 2, dtype=F32) / dim)
    ang = jnp.arange(seq, dtype=F32)[:, None] * inv[None, :]
    cos, sin = jnp.cos(ang), jnp.sin(ang)
    zeros = jnp.zeros_like(sin)
    pad_lo = jnp.zeros((seq, first_lane), F32)
    pad_hi = jnp.zeros((seq, period - first_lane - dim), F32)
    c = jnp.concatenate([pad_lo + 1.0, cos, cos, pad_hi + 1.0], axis=1)
    s_plus = jnp.concatenate([pad_lo, zeros, sin, pad_hi], axis=1)
    s_minus = jnp.concatenate([pad_lo, -sin, zeros, pad_hi], axis=1)
    reps = LANES // period
    return tuple(jnp.tile(t, (1, reps)) for t in (c, s_plus, s_minus))


def _mla_weights(w_in, w_uq, w_ukv):
    d = w_in.shape[0]
    ql, kvl = MLA_Q_LORA, MLA_KV_LORA
    w_kr = jnp.zeros((d, LANES), F32).at[:, MLA_NOPE:MLA_NOPE + MLA_ROPE].set(w_in[:, ql + kvl:])
    win = jnp.concatenate([w_in[:, :ql + kvl], w_kr], axis=1)
    wuq = w_uq.reshape(ql, N_HEADS, MLA_NOPE + MLA_ROPE)
    wuq = jnp.pad(wuq, ((0, 0), (0, 0), (0, MLA_QK_PAD - MLA_NOPE - MLA_ROPE))).reshape(ql, N_HEADS * MLA_QK_PAD)
    wukv = w_ukv.reshape(kvl, N_HEADS, MLA_NOPE + MLA_V)
    wuk = jnp.pad(wukv[:, :, :MLA_NOPE], ((0, 0), (0, 0), (0, MLA_QK_PAD - MLA_NOPE))).reshape(kvl, N_HEADS * MLA_QK_PAD)
    wuv = wukv[:, :, MLA_NOPE:].reshape(kvl, N_HEADS * MLA_V).T
    return win.astype(BF16), wuq.astype(BF16), wuk.astype(BF16), wuv.astype(BF16)


def kernel(x, norm_mix, norm_ffn, norm_final, a_w_qkv, a_w_o, b_w_in, b_q_norm, b_w_uq, b_kv_norm, b_w_ukv, b_w_o,
           c_w_qkv, c_w_o, d_w_qkv, d_w_o, ffn_w_gate, ffn_conv_w, ffn_conv_b, ffn_w_up, ffn_w_down):
    b, s, d = x.shape
    m = b * s
    hd = N_HEADS * HEAD_DIM
    depth = norm_mix.shape[0]
    rope_h = _rope_tables(s, HEAD_DIM, 0, HEAD_DIM)
    rope_r = _rope_tables(s, MLA_ROPE, MLA_NOPE, MLA_QK_PAD)
    h = x.reshape(m, d)
    for i in range(depth):
        kind, j = i % 4, i // 4
        g = norm_mix[i]
        if kind == 0:
            w = a_w_qkv[j].astype(BF16)
            qscale = jnp.concatenate([jnp.full((hd,), MOBA_Q_SCALE, F32), jnp.ones((hd,), F32)])
            qk = _norm_matmul(h, g, w[:, :2 * hd], s, rope=rope_h, half=HEAD_DIM // 2, colscale=qscale)
            vt = _norm_matmul_t(h, g, w[:, 2 * hd:].T, b, s)
            qk = qk.reshape(b, s, 2 * hd)
            o = _flash_attention(qk, qk, vt, True, 0, N_HEADS // 2)
            h = _out_proj(o.reshape(m, hd), a_w_o[j].astype(BF16), h)
        elif kind == 1:
            win, wuq, wuk, wuv = _mla_weights(b_w_in[j], b_w_uq[j], b_w_ukv[j])
            q, k, vt = _mla_proj(h, g, win, b_q_norm[j], b_kv_norm[j], wuq, wuk, wuv, rope_r, b, s)
            o = _flash_attention(q.reshape(b, s, -1), k.reshape(b, s, -1), vt, False, 0, 0)
            h = _out_proj(o.reshape(m, hd), b_w_o[j].astype(BF16), h)
        elif kind == 2:
            w = c_w_qkv[j].astype(BF16)
            dils = tuple(dil for _, dil in DIL_GROUPS)
            outs = [_dil_attention(_dil_proj(h, g, w[:, gi * 3 * hd:(gi + 1) * 3 * hd], rope_h, b, s, dil))
                    for gi, dil in enumerate(dils)]
            w_o = c_w_o[j].astype(BF16).reshape(N_HEADS // 2, 2, HEAD_DIM, d)
            zero = jnp.zeros_like(w_o[:, 0])
            w_o = jnp.concatenate([w_o[:, 0], zero, zero, w_o[:, 1]], axis=1).reshape(N_HEADS * 2 * HEAD_DIM, d)
            h = _dil_out_proj(outs, dils, w_o, h, s)
        else:
            qkv = _norm_matmul(h, g, d_w_qkv[j].astype(BF16), s).reshape(b, s, 3 * hd)
            h = _out_proj(_sb_attention(qkv).reshape(m, hd), d_w_o[j].astype(BF16), h)
        h = _ffn(h, norm_ffn[i], ffn_w_gate[i].astype(BF16), ffn_w_up[i].astype(BF16), ffn_w_down[i].astype(BF16),
                 ffn_conv_w[i], ffn_conv_b[i], s)
    return _final_norm(h, norm_final).reshape(b, s, d)
```

```python
import functools
import math

import jax
import jax.numpy as jnp
from jax import lax
from jax.experimental import pallas as pl
from jax.experimental.pallas import tpu as pltpu

D_MODEL = 1024
HEAD_DIM = 64
N_HEADS = 16
ROPE_THETA = 10000.0
RMS_EPS = 1e-6
NEG_INF = -1e30
MOBA_BLOCK = 256
MOBA_TOPK = 3
MLA_Q_LORA = 384
MLA_KV_LORA = 256
MLA_NOPE = 64
MLA_ROPE = 32
MLA_V = 64
MLA_QK_PAD = 128
DIL_GROUPS = ((128, 1), (512, 4), (2048, 16))
DIL_BLOCK = 128
D_FF = 2816
CONV_WIDTH = 3
LANES = 128
SB_UNDERFLOW = -104.0
VMEM_LIMIT = 56 * 1024 * 1024
LOG2E = 1.4426950408889634
MLA_Q_SCALE = (MLA_NOPE + MLA_ROPE) ** -0.5 * LOG2E
MOBA_Q_SCALE = HEAD_DIM ** -0.5 * LOG2E
FLASH_TQ = 512
FLASH_TK = 256
ONES_ROWS = 16
SB_PAIRS = 2
DIL_UNROLL = 4

F32 = jnp.float32
BF16 = jnp.bfloat16


def _cparams(sem):
    return pltpu.CompilerParams(dimension_semantics=sem, vmem_limit_bytes=VMEM_LIMIT)


def _dot(a, b):
    return jnp.dot(a, b, preferred_element_type=F32)


def _dot_nt(a, b):
    return lax.dot_general(a, b, (((1,), (1,)), ((), ())), preferred_element_type=F32)


def _rms(x, g):
    return x * lax.rsqrt(jnp.mean(x * x, axis=-1, keepdims=True) + RMS_EPS) * g


def _norm_matmul_kernel(*refs, rope, half, tn):
    if rope:
        x_ref, g_ref, w_ref, c_ref, sp_ref, sm_ref, scale_ref, o_ref = refs
    else:
        x_ref, g_ref, w_ref, o_ref = refs
    xn = _rms(x_ref[...], g_ref[...]).astype(BF16)
    y = _dot(xn, w_ref[...])
    if rope:
        reps = tn // LANES
        c = jnp.tile(c_ref[...], (1, reps))
        sp = jnp.tile(sp_ref[...], (1, reps))
        sm = jnp.tile(sm_ref[...], (1, reps))
        y = y * c + pltpu.roll(y, half, axis=1) * sp + pltpu.roll(y, tn - half, axis=1) * sm
        y = y * scale_ref[...]
    o_ref[...] = y.astype(o_ref.dtype)


def _norm_matmul(x, g, w, seq, *, rope=None, half=0, colscale=None, out_dtype=BF16, tm=512, tn=None):
    m, d = x.shape
    n = w.shape[1]
    tn = n if tn is None else tn
    grid = (n // tn, m // tm)
    in_specs = [pl.BlockSpec((tm, d), lambda j, i: (i, 0)),
                pl.BlockSpec((1, d), lambda j, i: (0, 0)),
                pl.BlockSpec((d, tn), lambda j, i: (0, j))]
    args = [x, g.reshape(1, d), w]
    if rope is not None:
        nblk = seq // tm
        for t in rope:
            in_specs.append(pl.BlockSpec((tm, LANES), lambda j, i: (i % nblk, 0)))
            args.append(t)
        in_specs.append(pl.BlockSpec((1, tn), lambda j, i: (0, j)))
        args.append(colscale.reshape(1, n))
    return pl.pallas_call(
        functools.partial(_norm_matmul_kernel, rope=rope is not None, half=half, tn=tn),
        grid=grid, in_specs=in_specs,
        out_specs=pl.BlockSpec((tm, tn), lambda j, i: (i, j)),
        out_shape=jax.ShapeDtypeStruct((m, n), out_dtype),
        compiler_params=_cparams(("parallel", "parallel")),
        name="norm_matmul",
    )(*args)


def _norm_matmul_t_kernel(x_ref, g_ref, wt_ref, o_ref):
    xn = _rms(x_ref[...], g_ref[...]).astype(BF16)
    o_ref[0] = _dot_nt(wt_ref[...], xn).astype(o_ref.dtype)


def _norm_matmul_t(x, g, wt, b, seq, tm=512):
    m, d = x.shape
    n = wt.shape[0]
    nblk = seq // tm
    return pl.pallas_call(
        _norm_matmul_t_kernel, grid=(m // tm,),
        in_specs=[pl.BlockSpec((tm, d), lambda i: (i, 0)), pl.BlockSpec((1, d), lambda i: (0, 0)),
                  pl.BlockSpec((n, d), lambda i: (0, 0))],
        out_specs=pl.BlockSpec((1, n, tm), lambda i: (i // nblk, 0, i % nblk)),
        out_shape=jax.ShapeDtypeStruct((b, n, seq), BF16),
        compiler_params=_cparams(("parallel",)),
        name="norm_matmul_t",
    )(x, g.reshape(1, d), wt)


def _mla_proj_kernel(x_ref, g_ref, win_ref, qn_ref, kvn_ref, wuq_ref, wuk_ref, wuv_ref,
                     c_ref, sp_ref, sm_ref, q_ref, k_ref, v_ref):
    half = MLA_ROPE // 2
    xn = _rms(x_ref[...], g_ref[...]).astype(BF16)
    c = _dot(xn, win_ref[...])
    cq = _rms(c[:, :MLA_Q_LORA], qn_ref[...]).astype(BF16)
    ckv = _rms(c[:, MLA_Q_LORA:MLA_Q_LORA + MLA_KV_LORA], kvn_ref[...]).astype(BF16)
    kr = c[:, MLA_Q_LORA + MLA_KV_LORA:]

    def rot(y, width):
        reps = width // LANES
        cc = jnp.tile(c_ref[...], (1, reps))
        sp = jnp.tile(sp_ref[...], (1, reps))
        sm = jnp.tile(sm_ref[...], (1, reps))
        return y * cc + pltpu.roll(y, half, axis=1) * sp + pltpu.roll(y, width - half, axis=1) * sm

    nq = wuq_ref.shape[1]
    q = rot(_dot(cq, wuq_ref[...]), nq) * MLA_Q_SCALE
    q_ref[...] = q.astype(q_ref.dtype)
    kr = rot(kr, LANES)
    k = _dot(ckv, wuk_ref[...]) + jnp.tile(kr, (1, nq // LANES))
    k_ref[...] = k.astype(k_ref.dtype)
    v_ref[0] = _dot_nt(wuv_ref[...], ckv).astype(v_ref.dtype)


def _mla_proj(x, g, win, qn, kvn, wuq, wuk, wuv, rope, b, seq, tm=512):
    m, d = x.shape
    nblk = seq // tm
    full = lambda a: pl.BlockSpec(a.shape, lambda i: (0, 0))
    qn2, kvn2, g2 = qn.reshape(1, -1), kvn.reshape(1, -1), g.reshape(1, -1)
    in_specs = [pl.BlockSpec((tm, d), lambda i: (i, 0)), full(g2), full(win), full(qn2), full(kvn2),
                full(wuq), full(wuk), full(wuv)]
    in_specs += [pl.BlockSpec((tm, LANES), lambda i: (i % nblk, 0))] * 3
    nq, nv = wuq.shape[1], wuv.shape[0]
    return pl.pallas_call(
        _mla_proj_kernel, grid=(m // tm,), in_specs=in_specs,
        out_specs=[pl.BlockSpec((tm, nq), lambda i: (i, 0)), pl.BlockSpec((tm, nq), lambda i: (i, 0)),
                   pl.BlockSpec((1, nv, tm), lambda i: (i // nblk, 0, i % nblk))],
        out_shape=[jax.ShapeDtypeStruct((m, nq), BF16), jax.ShapeDtypeStruct((m, nq), BF16),
                   jax.ShapeDtypeStruct((b, nv, seq), BF16)],
        compiler_params=_cparams(("parallel",)),
        name="mla_proj",
    )(x, g2, win, qn2, kvn2, wuq, wuk, wuv, *rope)


def _out_proj_kernel(a_ref, w_ref, h_ref, o_ref):
    o_ref[...] = h_ref[...] + _dot(a_ref[...], w_ref[...])


def _out_proj(a, w, h, tm=512):
    m, k = a.shape
    d = w.shape[1]
    return pl.pallas_call(
        _out_proj_kernel, grid=(m // tm,),
        in_specs=[pl.BlockSpec((tm, k), lambda i: (i, 0)), pl.BlockSpec((k, d), lambda i: (0, 0)),
                  pl.BlockSpec((tm, d), lambda i: (i, 0))],
        out_specs=pl.BlockSpec((tm, d), lambda i: (i, 0)),
        out_shape=jax.ShapeDtypeStruct((m, d), F32),
        compiler_params=_cparams(("parallel",)),
        name="out_proj",
    )(a, w, h)


def _gelu_exact(x):
    return 0.5 * x * (1.0 + lax.erf(x * (1.0 / math.sqrt(2.0))))


def _ffn_kernel(x_ref, g_ref, wg_ref, wu_ref, wd_ref, cw_ref, cb_ref, o_ref, xn_ref, acc_ref, carry_ref,
                *, nblk, fc, final_norm):
    i, c = pl.program_id(0), pl.program_id(1)
    nc = pl.num_programs(1)
    tm = x_ref.shape[0]

    @pl.when(c == 0)
    def _():
        xn_ref[...] = _rms(x_ref[...], g_ref[...]).astype(BF16)
        acc_ref[...] = jnp.zeros_like(acc_ref)

    @pl.when(jnp.logical_and(c == 0, i % nblk == 0))
    def _():
        carry_ref[...] = jnp.zeros_like(carry_ref)

    xn = xn_ref[...]
    gate = _dot(xn, wg_ref[...])
    up = _dot(xn, wu_ref[...])
    col = pl.multiple_of(c * fc, LANES)
    prev = carry_ref[:, pl.ds(col, fc)]
    row = lax.broadcasted_iota(jnp.int32, gate.shape, 0)
    p1, p2 = prev[7:8, :], prev[6:7, :]
    s1 = jnp.where(row == 0, p1, pltpu.roll(gate, 1, axis=0))
    s2 = jnp.where(row == 0, p2, jnp.where(row == 1, p1, pltpu.roll(gate, 2, axis=0)))
    carry_ref[:, pl.ds(col, fc)] = gate[tm - 8:, :]
    cw = cw_ref[0]
    gc = cb_ref[0] + cw[0:1, :] * s2 + cw[1:2, :] * s1 + cw[2:3, :] * gate
    hidden = (_gelu_exact(gc) * up).astype(BF16)
    acc_ref[...] += _dot(hidden, wd_ref[...])

    @pl.when(c == nc - 1)
    def _():
        o_ref[...] = x_ref[...] + acc_ref[...]


def _ffn(h, g, wg, wu, wd, cw, cb, seq, tm=512, fc=1408):
    m, d = h.shape
    f = wg.shape[1]
    nc = f // fc
    cw3 = cw.reshape(CONV_WIDTH, nc, fc).transpose(1, 0, 2)
    cb3 = cb.reshape(nc, 1, fc)
    return pl.pallas_call(
        functools.partial(_ffn_kernel, nblk=seq // tm, fc=fc, final_norm=False),
        grid=(m // tm, nc),
        in_specs=[pl.BlockSpec((tm, d), lambda i, c: (i, 0)),
                  pl.BlockSpec((1, d), lambda i, c: (0, 0)),
                  pl.BlockSpec((d, fc), lambda i, c: (0, c)),
                  pl.BlockSpec((d, fc), lambda i, c: (0, c)),
                  pl.BlockSpec((fc, d), lambda i, c: (c, 0)),
                  pl.BlockSpec((1, CONV_WIDTH, fc), lambda i, c: (c, 0, 0)),
                  pl.BlockSpec((1, 1, fc), lambda i, c: (c, 0, 0))],
        out_specs=pl.BlockSpec((tm, d), lambda i, c: (i, 0)),
        out_shape=jax.ShapeDtypeStruct((m, d), F32),
        scratch_shapes=[pltpu.VMEM((tm, d), BF16), pltpu.VMEM((tm, d), F32), pltpu.VMEM((8, f), F32)],
        compiler_params=_cparams(("arbitrary", "arbitrary")),
        name="conv_ffn",
    )(h, g.reshape(1, d), wg, wu, wd, cw3, cb3)


def _final_norm_kernel(x_ref, g_ref, o_ref):
    o_ref[...] = _rms(x_ref[...], g_ref[...])


def _final_norm(h, g, tm=1024):
    m, d = h.shape
    return pl.pallas_call(
        _final_norm_kernel, grid=(m // tm,),
        in_specs=[pl.BlockSpec((tm, d), lambda i: (i, 0)), pl.BlockSpec((1, d), lambda i: (0, 0))],
        out_specs=pl.BlockSpec((tm, d), lambda i: (i, 0)),
        out_shape=jax.ShapeDtypeStruct((m, d), F32),
        compiler_params=_cparams(("parallel",)),
        name="final_norm",
    )(h, g.reshape(1, d))


def _split_heads(x2):
    x = x2.astype(F32)
    low = lax.broadcasted_iota(jnp.int32, x.shape, 1) < HEAD_DIM
    return [jnp.where(low, x, 0.0).astype(BF16), jnp.where(low, 0.0, x).astype(BF16)]


def _flash_kernel(q_ref, k_ref, vt_ref, o_ref, *scratch, tq, tk, moba):
    i = pl.program_id(2)
    heads = range(2)
    s_ref = (scratch[0:2], scratch[2:4])
    p_ref = (scratch[4:6], scratch[6:8])
    acc_ref = scratch[8:10]
    st_ref = scratch[10]
    mt_ref = scratch[11]
    moba_refs = scratch[12:]
    dv = vt_ref.shape[1] // 2
    ones = jnp.ones((ONES_ROWS, tk), BF16)
    krow = lax.broadcasted_iota(jnp.int32, (tk, tq), 0)
    qcol = lax.broadcasted_iota(jnp.int32, (tk, tq), 1)

    if moba:
        kmean_ref, sel_ref = moba_refs
        nb = kmean_ref.shape[0]
        qs = _split_heads(q_ref[0])

        def ktile(start, hh):
            return k_ref[0, pl.ds(start, tk), :]

        @pl.when(i == 0)
        def _():
            for n in range(nb):
                rows = k_ref[0, n * tk:(n + 1) * tk, :].astype(F32)
                kmean_ref[n:n + 1, :] = jnp.sum(rows, axis=0, keepdims=True) * (1.0 / tk)

        km = kmean_ref[...]
        km_hi = km.astype(BF16)
        km_lo = (km - km_hi.astype(F32)).astype(BF16)
        blk = lax.broadcasted_iota(jnp.int32, (nb, tq), 0)
        lane = lax.broadcasted_iota(jnp.int32, (nb, tq), 1)
        qblk = i * (tq // tk)
        for t in range(1, tq // tk):
            qblk = qblk + jnp.where(lane >= t * tk, 1, 0)
        past = blk < qblk
        blk_f = blk.astype(F32)
        for hh in heads:
            gate = _dot_nt(km_hi, qs[hh]) + _dot_nt(km_lo, qs[hh])
            g = jnp.where(past, gate, NEG_INF)
            picked = jnp.zeros(gate.shape, F32)
            for _ in range(MOBA_TOPK):
                mx = jnp.max(g, axis=0, keepdims=True)
                first = jnp.min(jnp.where(g == mx, blk_f, float(nb)), axis=0, keepdims=True)
                hit = blk_f == first
                picked = jnp.where(hit, 1.0, picked)
                g = jnp.where(hit, -jnp.inf, g)
            sel_ref[hh] = jnp.where(past, picked, 0.0)
    else:
        dk = q_ref.shape[2] // 2
        qs = [q_ref[0, :, hh * dk:(hh + 1) * dk] for hh in heads]

        def ktile(start, hh):
            return k_ref[0, pl.ds(start, tk), hh * dk:(hh + 1) * dk]

    def mask(j, hh, s, off):
        causal = krow + off <= qcol
        if not moba:
            return jnp.where(causal, s, NEG_INF)
        same = jnp.where(jnp.logical_and(qcol >= off, qcol < off + tk), 1.0, 0.0)
        later = jnp.where(qcol >= off + tk, sel_ref[hh, pl.ds(j, 1), :], 0.0)
        allowed = jnp.where(same > 0.5, jnp.where(causal, 1.0, 0.0), later)
        return jnp.where(allowed > 0.5, s, NEG_INF)

    def put_scores(hh, slot, s):
        s_ref[hh][slot][...] = s
        mt_ref[2 * hh + slot:2 * hh + slot + 1, :] = jnp.max(s, axis=0, keepdims=True)

    def qk(j, hh):
        start = pl.multiple_of(j * tk, tk)
        return _dot_nt(ktile(start, hh), qs[hh])

    def pv(j, hh, p):
        start = pl.multiple_of(jnp.maximum(j, 0) * tk, tk)
        vt = jnp.concatenate([vt_ref[0, hh * dv:(hh + 1) * dv, pl.ds(start, tk)], ones], axis=0)
        return _dot(vt, p)

    def step(j, cur, off, last):
        if not last:
            for hh in heads:
                put_scores(hh, 1 - cur, qk(j + 1, hh))
        pvrs = [pv(j - 1, hh, p_ref[hh][cur][...]) for hh in heads]
        for hh in heads:
            m, alpha_prev = st_ref[2 * hh:2 * hh + 1, :], st_ref[2 * hh + 1:2 * hh + 2, :]
            if off is None:
                s = s_ref[hh][cur][...]
                tmax = mt_ref[2 * hh + cur:2 * hh + cur + 1, :]
                if moba:
                    chosen = sel_ref[hh, pl.ds(j, 1), :] > 0.5
                    m_new = jnp.maximum(m, jnp.where(chosen, tmax, NEG_INF))
                    shift = jnp.where(chosen, m_new, -NEG_INF)
                else:
                    m_new = jnp.maximum(m, tmax)
                    shift = m_new
            else:
                s = mask(j, hh, s_ref[hh][cur][...], off)
                m_new = jnp.maximum(m, jnp.max(s, axis=0, keepdims=True))
                shift = m_new
            alpha = jnp.exp2(m - m_new)
            p = jnp.exp2(s - shift)
            acc_ref[hh][...] = alpha_prev * acc_ref[hh][...] + pvrs[hh]
            p_ref[hh][1 - cur][...] = p.astype(BF16)
            st_ref[2 * hh:2 * hh + 1, :] = m_new
            st_ref[2 * hh + 1:2 * hh + 2, :] = alpha

    for hh in heads:
        put_scores(hh, 0, qk(0, hh))
        p_ref[hh][0][...] = jnp.zeros((tk, tq), BF16)
        acc_ref[hh][...] = jnp.zeros_like(acc_ref[hh])
        st_ref[2 * hh:2 * hh + 1, :] = jnp.full((1, tq), NEG_INF, F32)
        st_ref[2 * hh + 1:2 * hh + 2, :] = jnp.ones((1, tq), F32)

    def body(jj, carry):
        step(2 * jj, 0, None, False)
        step(2 * jj + 1, 1, None, False)
        return carry

    lax.fori_loop(0, i, body, 0)
    n_full = 2 * i
    step(n_full, 0, 0, False)
    step(n_full + 1, 1, tk, True)
    outs = []
    for hh in heads:
        acc = st_ref[2 * hh + 1:2 * hh + 2, :] * acc_ref[hh][...] + pv(n_full + 1, hh, p_ref[hh][0][...])
        outs.append(acc[:dv] / acc[dv:dv + 1])
    o_ref[0] = jnp.concatenate(outs, axis=0).T.astype(o_ref.dtype)


def _flash_attention(q_arr, k_arr, vt, moba, q_blk0, k_blk0):
    b, s, _ = q_arr.shape
    dv = vt.shape[1] // N_HEADS
    qw = 2 * HEAD_DIM if moba else 2 * MLA_QK_PAD
    tq, tk = FLASH_TQ, FLASH_TK
    assert tq == 2 * tk and s % tq == 0 and (not moba or tk == MOBA_BLOCK)
    scratch = ([pltpu.VMEM((tk, tq), F32)] * 4 + [pltpu.VMEM((tk, tq), BF16)] * 4
               + [pltpu.VMEM((dv + ONES_ROWS, tq), F32)] * 2 + [pltpu.VMEM((8, tq), F32)] * 2)
    if moba:
        scratch += [pltpu.VMEM((s // tk, qw), F32), pltpu.VMEM((2, s // tk, tq), F32)]
    return pl.pallas_call(
        functools.partial(_flash_kernel, tq=tq, tk=tk, moba=moba),
        grid=(b, N_HEADS // 2, s // tq),
        in_specs=[pl.BlockSpec((1, tq, qw), lambda b_, h_, i: (b_, i, q_blk0 + h_)),
                  pl.BlockSpec((1, s, qw), lambda b_, h_, i: (b_, 0, k_blk0 + h_)),
                  pl.BlockSpec((1, 2 * dv, s), lambda b_, h_, i: (b_, h_, 0))],
        out_specs=pl.BlockSpec((1, tq, 2 * dv), lambda b_, h_, i: (b_, i, h_)),
        out_shape=jax.ShapeDtypeStruct((b, s, N_HEADS * dv), BF16),
        scratch_shapes=scratch,
        compiler_params=_cparams(("parallel", "parallel", "arbitrary")),
        name="moba_attention" if moba else "mla_attention",
    )(q_arr, k_arr, vt)


def _sb_kernel(q_ref, k_ref, v_ref, o_ref, acc_ref, stay_ref, *, t):
    i = pl.program_id(2)
    w = 2 * HEAD_DIM
    heads = range(2 * SB_PAIRS)
    r = lax.broadcasted_iota(jnp.int32, (t, t), 0)
    c = lax.broadcasted_iota(jnp.int32, (t, t), 1)
    later = jnp.where(r > c, 1.0, 0.0).astype(BF16)
    past = c < r
    qs = []
    for pp in range(SB_PAIRS):
        qs += _split_heads(q_ref[0, :, pp * w:(pp + 1) * w] * 0.125)

    def tile(j, diag):
        start = pl.multiple_of(j * t, t)
        cols = [slice((hh // 2) * w, (hh // 2 + 1) * w) for hh in heads]
        zs = [_dot_nt(qs[hh], k_ref[0, pl.ds(start, t), cols[hh]]) for hh in heads]
        parts = []
        for z in zs:
            lp = jnp.log(1.0 + jnp.exp(-jnp.abs(z)))
            log_beta = jnp.minimum(z, 0.0) - lp
            log_stay = log_beta - z
            if diag:
                log_stay = jnp.where(past, log_stay, 0.0)
            hi = log_stay.astype(BF16)
            lo = (log_stay - hi.astype(F32)).astype(BF16)
            parts.append((log_beta, log_stay, hi, lo))
        inners = [_dot(hi, later) + _dot(lo, later) for _, _, hi, lo in parts]
        weights = []
        top = None
        for hh in heads:
            log_beta, log_stay, _, _ = parts[hh]
            a = jnp.exp(log_beta + inners[hh] + stay_ref[hh])
            if diag:
                a = jnp.where(past, a, 0.0)
            weights.append(a.astype(BF16))
            stay = stay_ref[hh] + jnp.sum(log_stay, axis=1, keepdims=True)
            stay_ref[hh] = stay
            mx = jnp.max(stay)
            top = mx if top is None else jnp.maximum(top, mx)
        for hh in heads:
            acc_ref[hh] += _dot(weights[hh], v_ref[0, pl.ds(start, t), cols[hh]])
        return top

    acc_ref[...] = jnp.zeros_like(acc_ref)
    stay_ref[...] = jnp.zeros_like(stay_ref)
    top = tile(i, True)

    def cond(state):
        j, top = state
        return jnp.logical_and(j >= 0, top > SB_UNDERFLOW)

    def body(state):
        j, _ = state
        return j - 1, tile(j, False)

    lax.while_loop(cond, body, (i - 1, top))
    low = lax.broadcasted_iota(jnp.int32, (t, w), 1) < HEAD_DIM
    for pp in range(SB_PAIRS):
        o_ref[0, :, pp * w:(pp + 1) * w] = jnp.where(low, acc_ref[2 * pp], acc_ref[2 * pp + 1]).astype(o_ref.dtype)


def _sb_attention(qkv, t=256):
    b, s, _ = qkv.shape
    w = 2 * HEAD_DIM * SB_PAIRS
    npair = N_HEADS // (2 * SB_PAIRS)
    return pl.pallas_call(
        functools.partial(_sb_kernel, t=t),
        grid=(b, npair, s // t),
        in_specs=[pl.BlockSpec((1, t, w), lambda b_, h_, i: (b_, i, h_)),
                  pl.BlockSpec((1, s, w), lambda b_, h_, i: (b_, 0, npair + h_)),
                  pl.BlockSpec((1, s, w), lambda b_, h_, i: (b_, 0, 2 * npair + h_))],
        out_specs=pl.BlockSpec((1, t, w), lambda b_, h_, i: (b_, i, h_)),
        out_shape=jax.ShapeDtypeStruct((b, s, N_HEADS * HEAD_DIM), BF16),
        scratch_shapes=[pltpu.VMEM((2 * SB_PAIRS, t, 2 * HEAD_DIM), F32), pltpu.VMEM((2 * SB_PAIRS, t, 1), F32)],
        compiler_params=_cparams(("parallel", "parallel", "arbitrary")),
        name="sb_attention",
    )(qkv, qkv, qkv)


def _dil_proj_kernel(x_ref, g_ref, w_ref, c_ref, sp_ref, sm_ref, o_ref, *scratch, dil, nrope, half):
    tm = x_ref.shape[0]
    xn = _rms(x_ref[...], g_ref[...]).astype(BF16)
    y = _dot(xn, w_ref[...])
    reps = nrope // LANES
    cc = jnp.tile(c_ref[...], (1, reps))
    sp = jnp.tile(sp_ref[...], (1, reps))
    sm = jnp.tile(sm_ref[...], (1, reps))
    yr = y[:, :nrope]
    yr = yr * cc + pltpu.roll(yr, half, axis=1) * sp + pltpu.roll(yr, nrope - half, axis=1) * sm
    if dil == 1:
        o_ref[0, 0, :, :nrope] = yr.astype(o_ref.dtype)
        o_ref[0, 0, :, nrope:] = y[:, nrope:].astype(o_ref.dtype)
    else:
        y_ref, = scratch
        for cb in range(y.shape[1] // LANES):
            cols = slice(cb * LANES, (cb + 1) * LANES)
            y_ref[cb] = yr[:, cols] if cb < nrope // LANES else y[:, cols]
            for r in range(dil):
                o_ref[0, r, :, cols] = y_ref[cb, pl.ds(r, tm // dil, stride=dil), :].astype(o_ref.dtype)


def _dil_proj(x, g, w, rope, b, seq, dil, tm=512):
    m, d = x.shape
    n = w.shape[1]
    nblk = seq // tm
    nrope = 2 * N_HEADS * HEAD_DIM
    in_specs = [pl.BlockSpec((tm, d), lambda i: (i, 0)), pl.BlockSpec((1, d), lambda i: (0, 0)),
                pl.BlockSpec((d, n), lambda i: (0, 0))]
    in_specs += [pl.BlockSpec((tm, LANES), lambda i: (i % nblk, 0))] * 3
    return pl.pallas_call(
        functools.partial(_dil_proj_kernel, dil=dil, nrope=nrope, half=HEAD_DIM // 2),
        grid=(m // tm,), in_specs=in_specs,
        out_specs=pl.BlockSpec((1, dil, tm // dil, n), lambda i: (i // nblk, 0, i % nblk, 0)),
        out_shape=jax.ShapeDtypeStruct((b, dil, seq // dil, n), BF16),
        scratch_shapes=[] if dil == 1 else [pltpu.VMEM((n // LANES, tm, LANES), F32)],
        compiler_params=_cparams(("parallel",)),
        name="dil_proj",
    )(x, g.reshape(1, d), w, *rope)


def _dil_kernel(q_ref, k_ref, v_ref, o_ref, *, nblk, unroll):
    t = DIL_BLOCK
    r = lax.broadcasted_iota(jnp.int32, (t, t), 0)
    c = lax.broadcasted_iota(jnp.int32, (t, t), 1)
    low = lax.broadcasted_iota(jnp.int32, (t, 2 * HEAD_DIM), 1) < HEAD_DIM

    def body(nn, _):
        scores, values = [], []
        for u in range(unroll):
            n = nn * unroll + u
            start = pl.multiple_of(n * t, t)
            prev = pl.multiple_of(jnp.maximum(n - 1, 0) * t, t)
            first = jnp.where(n > 0, 0, t)
            qs = _split_heads(q_ref[0, 0, pl.ds(start, t), :] * 0.125)
            k_cur, k_prev = k_ref[0, 0, pl.ds(start, t), :], k_ref[0, 0, pl.ds(prev, t), :]
            values.append((start, v_ref[0, 0, pl.ds(start, t), :], v_ref[0, 0, pl.ds(prev, t), :]))
            for hh in range(2):
                s_cur = jnp.where(c <= r, _dot_nt(qs[hh], k_cur), NEG_INF)
                s_prev = jnp.where(c >= r + first, _dot_nt(qs[hh], k_prev), NEG_INF)
                scores.append((s_cur, s_prev))
        probs = []
        for s_cur, s_prev in scores:
            m = jnp.max(jnp.maximum(s_cur, s_prev), axis=1, keepdims=True)
            p_cur = jnp.exp(s_cur - m)
            p_prev = jnp.exp(s_prev - m)
            l = jnp.sum(p_cur + p_prev, axis=1, keepdims=True)
            probs.append((p_cur.astype(BF16), p_prev.astype(BF16), l, m + jnp.log(l)))
        for u, (start, v_cur, v_prev) in enumerate(values):
            outs = []
            for hh in range(2):
                p_cur, p_prev, l, lse = probs[2 * u + hh]
                acc = _dot(p_cur, v_cur) + _dot(p_prev, v_prev)
                lse = jnp.broadcast_to(lse, acc.shape)
                outs.append(jnp.where(low, acc / l, lse) if hh == 0 else jnp.where(low, lse, acc / l))
            o_ref[0, 0, pl.ds(start, t), :] = jnp.concatenate(outs, axis=1)
        return 0

    lax.fori_loop(0, nblk // unroll, body, 0)


def _dil_attention(qkv):
    b, dil, length, _ = qkv.shape
    nblk = length // DIL_BLOCK
    w = 2 * HEAD_DIM
    npair = N_HEADS // 2

    def spec(first):
        return pl.BlockSpec((1, 1, length, w), lambda b_, r_, h_: (b_, r_, 0, first + h_))

    return pl.pallas_call(
        functools.partial(_dil_kernel, nblk=nblk, unroll=math.gcd(DIL_UNROLL, nblk)),
        grid=(b, dil, npair),
        in_specs=[spec(0), spec(npair), spec(2 * npair)],
        out_specs=pl.BlockSpec((1, 1, length, 2 * w), lambda b_, r_, h_: (b_, r_, 0, h_)),
        out_shape=jax.ShapeDtypeStruct((b, dil, length, N_HEADS * w), F32),
        compiler_params=_cparams(("parallel", "parallel", "parallel")),
        name="dil_attention",
    )(qkv, qkv, qkv)


def _dil_out_proj_kernel(a0_ref, a1_ref, a2_ref, w_ref, h_ref, o_ref, *scratch, dils):
    tm = h_ref.shape[0]
    xs = []
    scratch = list(scratch)
    for a_ref, dil in zip((a0_ref, a1_ref, a2_ref), dils):
        if dil == 1:
            xs.append(a_ref[0, 0])
        else:
            y_ref = scratch.pop(0)
            ncb = y_ref.shape[0]
            for cb in range(ncb):
                for r in range(dil):
                    y_ref[cb, pl.ds(r, tm // dil, stride=dil), :] = a_ref[0, r, :, cb * LANES:(cb + 1) * LANES]
            xs.append(jnp.concatenate([y_ref[cb] for cb in range(ncb)], axis=1))
    width = xs[0].shape[1]
    lane = lax.broadcasted_iota(jnp.int32, xs[0].shape, 1)
    low = lane % LANES < HEAD_DIM
    even = lane % (2 * LANES) < LANES
    mx = jnp.maximum(jnp.maximum(xs[0], xs[1]), xs[2])
    ws = [jnp.exp(x - mx) for x in xs]
    den = ws[0] + ws[1] + ws[2]
    comb = jnp.zeros_like(xs[0])
    for x, wgt in zip(xs, ws):
        wn = wgt / den
        wn = jnp.where(even, pltpu.roll(wn, width - HEAD_DIM, axis=1), pltpu.roll(wn, HEAD_DIM, axis=1))
        comb = comb + wn * x
    is_out = jnp.where(even, jnp.where(low, 1.0, 0.0), jnp.where(low, 0.0, 1.0))
    comb = jnp.where(is_out > 0.5, comb, 0.0).astype(BF16)
    o_ref[...] = h_ref[...] + _dot(comb, w_ref[...])


def _dil_out_proj(a_list, dils, w_exp, h, seq, tm=256):
    m, d = h.shape
    k = w_exp.shape[0]
    nblk = seq // tm
    in_specs = [pl.BlockSpec((1, dil, tm // dil, k), lambda i: (i // nblk, 0, i % nblk, 0)) for dil in dils]
    in_specs += [pl.BlockSpec((k, d), lambda i: (0, 0)), pl.BlockSpec((tm, d), lambda i: (i, 0))]
    return pl.pallas_call(
        functools.partial(_dil_out_proj_kernel, dils=dils), grid=(m // tm,),
        in_specs=in_specs,
        out_specs=pl.BlockSpec((tm, d), lambda i: (i, 0)),
        out_shape=jax.ShapeDtypeStruct((m, d), F32),
        scratch_shapes=[pltpu.VMEM((k // LANES, tm, LANES), F32) for dil in dils if dil > 1],
        compiler_params=_cparams(("parallel",)),
        name="dil_out_proj",
    )(*a_list, w_exp, h)


def _rope_tables(seq, dim, first_lane, period):
    half = dim // 2
    inv = ROPE_THETA ** (-jnp.arange(0, dim, 2, dtype=F32) / dim)
    ang = jnp.arange(seq, dtype=F32)[:, None] * inv[None, :]
    cos, sin = jnp.cos(ang), jnp.sin(ang)
    zeros = jnp.zeros_like(sin)
    pad_lo = jnp.zeros((seq, first_lane), F32)
    pad_hi = jnp.zeros((seq, period - first_lane - dim), F32)
    c = jnp.concatenate([pad_lo + 1.0, cos, cos, pad_hi + 1.0], axis=1)
    s_plus = jnp.concatenate([pad_lo, zeros, sin, pad_hi], axis=1)
    s_minus = jnp.concatenate([pad_lo, -sin, zeros, pad_hi], axis=1)
    reps = LANES // period
    return tuple(jnp.tile(t, (1, reps)) for t in (c, s_plus, s_minus))


def _mla_weights(w_in, w_uq, w_ukv):
    d = w_in.shape[0]
    ql, kvl = MLA_Q_LORA, MLA_KV_LORA
    w_kr = jnp.zeros((d, LANES), F32).at[:, MLA_NOPE:MLA_NOPE + MLA_ROPE].set(w_in[:, ql + kvl:])
    win = jnp.concatenate([w_in[:, :ql + kvl], w_kr], axis=1)
    wuq = w_uq.reshape(ql, N_HEADS, MLA_NOPE + MLA_ROPE)
    wuq = jnp.pad(wuq, ((0, 0), (0, 0), (0, MLA_QK_PAD - MLA_NOPE - MLA_ROPE))).reshape(ql, N_HEADS * MLA_QK_PAD)
    wukv = w_ukv.reshape(kvl, N_HEADS, MLA_NOPE + MLA_V)
    wuk = jnp.pad(wukv[:, :, :MLA_NOPE], ((0, 0), (0, 0), (0, MLA_QK_PAD - MLA_NOPE))).reshape(kvl, N_HEADS * MLA_QK_PAD)
    wuv = wukv[:, :, MLA_NOPE:].reshape(kvl, N_HEADS * MLA_V).T
    return win.astype(BF16), wuq.astype(BF16), wuk.astype(BF16), wuv.astype(BF16)


def kernel(x, norm_mix, norm_ffn, norm_final, a_w_qkv, a_w_o, b_w_in, b_q_norm, b_w_uq, b_kv_norm, b_w_ukv, b_w_o,
           c_w_qkv, c_w_o, d_w_qkv, d_w_o, ffn_w_gate, ffn_conv_w, ffn_conv_b, ffn_w_up, ffn_w_down):
    b, s, d = x.shape
    m = b * s
    hd = N_HEADS * HEAD_DIM
    depth = norm_mix.shape[0]
    rope_h = _rope_tables(s, HEAD_DIM, 0, HEAD_DIM)
    rope_r = _rope_tables(s, MLA_ROPE, MLA_NOPE, MLA_QK_PAD)
    h = x.reshape(m, d)
    for i in range(depth):
        kind, j = i % 4, i // 4
        g = norm_mix[i]
        if kind == 0:
            w = a_w_qkv[j].astype(BF16)
            qscale = jnp.concatenate([jnp.full((hd,), MOBA_Q_SCALE, F32), jnp.ones((hd,), F32)])
            qk = _norm_matmul(h, g, w[:, :2 * hd], s, rope=rope_h, half=HEAD_DIM // 2, colscale=qscale)
            vt = _norm_matmul_t(h, g, w[:, 2 * hd:].T, b, s)
            qk = qk.reshape(b, s, 2 * hd)
            o = _flash_attention(qk, qk, vt, True, 0, N_HEADS // 2)
            h = _out_proj(o.reshape(m, hd), a_w_o[j].astype(BF16), h)
        elif kind == 1:
            win, wuq, wuk, wuv = _mla_weights(b_w_in[j], b_w_uq[j], b_w_ukv[j])
            q, k, vt = _mla_proj(h, g, win, b_q_norm[j], b_kv_norm[j], wuq, wuk, wuv, rope_r, b, s)
            o = _flash_attention(q.reshape(b, s, -1), k.reshape(b, s, -1), vt, False, 0, 0)
            h = _out_proj(o.reshape(m, hd), b_w_o[j].astype(BF16), h)
        elif kind == 2:
            w = c_w_qkv[j].astype(BF16)
            dils = tuple(dil for _, dil in DIL_GROUPS)
            outs = [_dil_attention(_dil_proj(h, g, w[:, gi * 3 * hd:(gi + 1) * 3 * hd], rope_h, b, s, dil))
                    for gi, dil in enumerate(dils)]
            w_o = c_w_o[j].astype(BF16).reshape(N_HEADS // 2, 2, HEAD_DIM, d)
            zero = jnp.zeros_like(w_o[:, 0])
            w_o = jnp.concatenate([w_o[:, 0], zero, zero, w_o[:, 1]], axis=1).reshape(N_HEADS * 2 * HEAD_DIM, d)
            h = _dil_out_proj(outs, dils, w_o, h, s)
        else:
            qkv = _norm_matmul(h, g, d_w_qkv[j].astype(BF16), s).reshape(b, s, 3 * hd)
            h = _out_proj(_sb_attention(qkv).reshape(m, hd), d_w_o[j].astype(BF16), h)
        h = _ffn(h, norm_ffn[i], ffn_w_gate[i].astype(BF16), ffn_w_up[i].astype(BF16), ffn_w_down[i].astype(BF16),
                 ffn_conv_w[i], ffn_conv_b[i], s)
    return _final_norm(h, norm_final).reshape(b, s, d)
```

```python
import functools
import math

import jax
import jax.numpy as jnp
from jax import lax
from jax.experimental import pallas as pl
from jax.experimental.pallas import tpu as pltpu

D_MODEL = 1024
HEAD_DIM = 64
N_HEADS = 16
ROPE_THETA = 10000.0
RMS_EPS = 1e-6
NEG_INF = -1e30
MOBA_BLOCK = 256
MOBA_TOPK = 3
MLA_Q_LORA = 384
MLA_KV_LORA = 256
MLA_NOPE = 64
MLA_ROPE = 32
MLA_V = 64
MLA_QK_PAD = 128
DIL_GROUPS = ((128, 1), (512, 4), (2048, 16))
DIL_BLOCK = 128
D_FF = 2816
CONV_WIDTH = 3
LANES = 128
SUBLANES = 8
SB_UNDERFLOW = -104.0
VMEM_LIMIT = 56 * 1024 * 1024
LOG2E = 1.4426950408889634
MLA_Q_SCALE = (MLA_NOPE + MLA_ROPE) ** -0.5 * LOG2E
MOBA_Q_SCALE = HEAD_DIM ** -0.5 * LOG2E
FLASH_TQ = 512
FLASH_TK = 256
FLASH_PAIRS = 2
ONES_ROWS = 16
SB_PAIRS = 2
DIL_UNROLL = 4

F32 = jnp.float32
BF16 = jnp.bfloat16


def _cparams(sem):
    return pltpu.CompilerParams(dimension_semantics=sem, vmem_limit_bytes=VMEM_LIMIT)


def _dot(a, b):
    return jnp.dot(a, b, preferred_element_type=F32)


def _dot_nt(a, b):
    return lax.dot_general(a, b, (((1,), (1,)), ((), ())), preferred_element_type=F32)


def _rms(x, g):
    return x * lax.rsqrt(jnp.mean(x * x, axis=-1, keepdims=True) + RMS_EPS) * g


def _norm_matmul_kernel(*refs, rope, half, tn):
    if rope:
        x_ref, g_ref, w_ref, c_ref, sp_ref, sm_ref, scale_ref, o_ref = refs
    else:
        x_ref, g_ref, w_ref, o_ref = refs
    xn = _rms(x_ref[...], g_ref[...]).astype(BF16)
    y = _dot(xn, w_ref[...])
    if rope:
        reps = tn // LANES
        c = jnp.tile(c_ref[...], (1, reps))
        sp = jnp.tile(sp_ref[...], (1, reps))
        sm = jnp.tile(sm_ref[...], (1, reps))
        y = y * c + pltpu.roll(y, half, axis=1) * sp + pltpu.roll(y, tn - half, axis=1) * sm
        y = y * scale_ref[...]
    o_ref[...] = y.astype(o_ref.dtype)


def _norm_matmul(x, g, w, seq, *, rope=None, half=0, colscale=None, out_dtype=BF16, tm=512, tn=None):
    m, d = x.shape
    n = w.shape[1]
    tn = n if tn is None else tn
    grid = (n // tn, m // tm)
    in_specs = [pl.BlockSpec((tm, d), lambda j, i: (i, 0)),
                pl.BlockSpec((1, d), lambda j, i: (0, 0)),
                pl.BlockSpec((d, tn), lambda j, i: (0, j))]
    args = [x, g.reshape(1, d), w]
    if rope is not None:
        nblk = seq // tm
        for t in rope:
            in_specs.append(pl.BlockSpec((tm, LANES), lambda j, i: (i % nblk, 0)))
            args.append(t)
        in_specs.append(pl.BlockSpec((1, tn), lambda j, i: (0, j)))
        args.append(colscale.reshape(1, n))
    return pl.pallas_call(
        functools.partial(_norm_matmul_kernel, rope=rope is not None, half=half, tn=tn),
        grid=grid, in_specs=in_specs,
        out_specs=pl.BlockSpec((tm, tn), lambda j, i: (i, j)),
        out_shape=jax.ShapeDtypeStruct((m, n), out_dtype),
        compiler_params=_cparams(("parallel", "parallel")),
        name="norm_matmul",
    )(*args)


def _norm_matmul_t_kernel(x_ref, g_ref, wt_ref, o_ref):
    xn = _rms(x_ref[...], g_ref[...]).astype(BF16)
    o_ref[0] = _dot_nt(wt_ref[...], xn).astype(o_ref.dtype)


def _norm_matmul_t(x, g, wt, b, seq, tm=512):
    m, d = x.shape
    n = wt.shape[0]
    nblk = seq // tm
    return pl.pallas_call(
        _norm_matmul_t_kernel, grid=(m // tm,),
        in_specs=[pl.BlockSpec((tm, d), lambda i: (i, 0)), pl.BlockSpec((1, d), lambda i: (0, 0)),
                  pl.BlockSpec((n, d), lambda i: (0, 0))],
        out_specs=pl.BlockSpec((1, n, tm), lambda i: (i // nblk, 0, i % nblk)),
        out_shape=jax.ShapeDtypeStruct((b, n, seq), BF16),
        compiler_params=_cparams(("parallel",)),
        name="norm_matmul_t",
    )(x, g.reshape(1, d), wt)


def _mla_proj_kernel(x_ref, g_ref, win_ref, qn_ref, kvn_ref, wuq_ref, wuk_ref, wuv_ref,
                     c_ref, sp_ref, sm_ref, q_ref, k_ref, v_ref):
    half = MLA_ROPE // 2
    xn = _rms(x_ref[...], g_ref[...]).astype(BF16)
    c = _dot(xn, win_ref[...])
    cq = _rms(c[:, :MLA_Q_LORA], qn_ref[...]).astype(BF16)
    ckv = _rms(c[:, MLA_Q_LORA:MLA_Q_LORA + MLA_KV_LORA], kvn_ref[...]).astype(BF16)
    kr = c[:, MLA_Q_LORA + MLA_KV_LORA:]

    def rot(y, width):
        reps = width // LANES
        cc = jnp.tile(c_ref[...], (1, reps))
        sp = jnp.tile(sp_ref[...], (1, reps))
        sm = jnp.tile(sm_ref[...], (1, reps))
        return y * cc + pltpu.roll(y, half, axis=1) * sp + pltpu.roll(y, width - half, axis=1) * sm

    nq = wuq_ref.shape[1]
    q = rot(_dot(cq, wuq_ref[...]), nq) * MLA_Q_SCALE
    q_ref[...] = q.astype(q_ref.dtype)
    kr = rot(kr, LANES)
    k = _dot(ckv, wuk_ref[...]) + jnp.tile(kr, (1, nq // LANES))
    k_ref[...] = k.astype(k_ref.dtype)
    v_ref[0] = _dot_nt(wuv_ref[...], ckv).astype(v_ref.dtype)


def _mla_proj(x, g, win, qn, kvn, wuq, wuk, wuv, rope, b, seq, tm=512):
    m, d = x.shape
    nblk = seq // tm
    full = lambda a: pl.BlockSpec(a.shape, lambda i: (0, 0))
    qn2, kvn2, g2 = qn.reshape(1, -1), kvn.reshape(1, -1), g.reshape(1, -1)
    in_specs = [pl.BlockSpec((tm, d), lambda i: (i, 0)), full(g2), full(win), full(qn2), full(kvn2),
                full(wuq), full(wuk), full(wuv)]
    in_specs += [pl.BlockSpec((tm, LANES), lambda i: (i % nblk, 0))] * 3
    nq, nv = wuq.shape[1], wuv.shape[0]
    return pl.pallas_call(
        _mla_proj_kernel, grid=(m // tm,), in_specs=in_specs,
        out_specs=[pl.BlockSpec((tm, nq), lambda i: (i, 0)), pl.BlockSpec((tm, nq), lambda i: (i, 0)),
                   pl.BlockSpec((1, nv, tm), lambda i: (i // nblk, 0, i % nblk))],
        out_shape=[jax.ShapeDtypeStruct((m, nq), BF16), jax.ShapeDtypeStruct((m, nq), BF16),
                   jax.ShapeDtypeStruct((b, nv, seq), BF16)],
        compiler_params=_cparams(("parallel",)),
        name="mla_proj",
    )(x, g2, win, qn2, kvn2, wuq, wuk, wuv, *rope)


def _out_proj_kernel(a_ref, w_ref, h_ref, o_ref):
    o_ref[...] = h_ref[...] + _dot(a_ref[...], w_ref[...])


def _out_proj(a, w, h, tm=512):
    m, k = a.shape
    d = w.shape[1]
    return pl.pallas_call(
        _out_proj_kernel, grid=(m // tm,),
        in_specs=[pl.BlockSpec((tm, k), lambda i: (i, 0)), pl.BlockSpec((k, d), lambda i: (0, 0)),
                  pl.BlockSpec((tm, d), lambda i: (i, 0))],
        out_specs=pl.BlockSpec((tm, d), lambda i: (i, 0)),
        out_shape=jax.ShapeDtypeStruct((m, d), F32),
        compiler_params=_cparams(("parallel",)),
        name="out_proj",
    )(a, w, h)


def _gelu_exact(x):
    return 0.5 * x * (1.0 + lax.erf(x * (1.0 / math.sqrt(2.0))))


def _ffn_kernel(x_ref, g_ref, wg_ref, wu_ref, wd_ref, cw_ref, cb_ref, o_ref, xn_ref, acc_ref, carry_ref,
                *, nblk, fc, final_norm):
    i, c = pl.program_id(0), pl.program_id(1)
    nc = pl.num_programs(1)
    tm = x_ref.shape[0]

    @pl.when(c == 0)
    def _():
        xn_ref[...] = _rms(x_ref[...], g_ref[...]).astype(BF16)
        acc_ref[...] = jnp.zeros_like(acc_ref)

    @pl.when(jnp.logical_and(c == 0, i % nblk == 0))
    def _():
        carry_ref[...] = jnp.zeros_like(carry_ref)

    xn = xn_ref[...]
    gate = _dot(xn, wg_ref[...])
    up = _dot(xn, wu_ref[...])
    col = pl.multiple_of(c * fc, LANES)
    prev = carry_ref[:, pl.ds(col, fc)]
    row = lax.broadcasted_iota(jnp.int32, gate.shape, 0)
    p1, p2 = prev[7:8, :], prev[6:7, :]
    s1 = jnp.where(row == 0, p1, pltpu.roll(gate, 1, axis=0))
    s2 = jnp.where(row == 0, p2, jnp.where(row == 1, p1, pltpu.roll(gate, 2, axis=0)))
    carry_ref[:, pl.ds(col, fc)] = gate[tm - 8:, :]
    cw = cw_ref[0]
    gc = cb_ref[0] + cw[0:1, :] * s2 + cw[1:2, :] * s1 + cw[2:3, :] * gate
    hidden = (_gelu_exact(gc) * up).astype(BF16)
    acc_ref[...] += _dot(hidden, wd_ref[...])

    @pl.when(c == nc - 1)
    def _():
        o_ref[...] = x_ref[...] + acc_ref[...]


def _ffn(h, g, wg, wu, wd, cw, cb, seq, tm=512, fc=1408):
    m, d = h.shape
    f = wg.shape[1]
    nc = f // fc
    cw3 = cw.reshape(CONV_WIDTH, nc, fc).transpose(1, 0, 2)
    cb3 = cb.reshape(nc, 1, fc)
    return pl.pallas_call(
        functools.partial(_ffn_kernel, nblk=seq // tm, fc=fc, final_norm=False),
        grid=(m // tm, nc),
        in_specs=[pl.BlockSpec((tm, d), lambda i, c: (i, 0)),
                  pl.BlockSpec((1, d), lambda i, c: (0, 0)),
                  pl.BlockSpec((d, fc), lambda i, c: (0, c)),
                  pl.BlockSpec((d, fc), lambda i, c: (0, c)),
                  pl.BlockSpec((fc, d), lambda i, c: (c, 0)),
                  pl.BlockSpec((1, CONV_WIDTH, fc), lambda i, c: (c, 0, 0)),
                  pl.BlockSpec((1, 1, fc), lambda i, c: (c, 0, 0))],
        out_specs=pl.BlockSpec((tm, d), lambda i, c: (i, 0)),
        out_shape=jax.ShapeDtypeStruct((m, d), F32),
        scratch_shapes=[pltpu.VMEM((tm, d), BF16), pltpu.VMEM((tm, d), F32), pltpu.VMEM((8, f), F32)],
        compiler_params=_cparams(("arbitrary", "arbitrary")),
        name="conv_ffn",
    )(h, g.reshape(1, d), wg, wu, wd, cw3, cb3)


def _final_norm_kernel(x_ref, g_ref, o_ref):
    o_ref[...] = _rms(x_ref[...], g_ref[...])


def _final_norm(h, g, tm=1024):
    m, d = h.shape
    return pl.pallas_call(
        _final_norm_kernel, grid=(m // tm,),
        in_specs=[pl.BlockSpec((tm, d), lambda i: (i, 0)), pl.BlockSpec((1, d), lambda i: (0, 0))],
        out_specs=pl.BlockSpec((tm, d), lambda i: (i, 0)),
        out_shape=jax.ShapeDtypeStruct((m, d), F32),
        compiler_params=_cparams(("parallel",)),
        name="final_norm",
    )(h, g.reshape(1, d))


def _split_heads(x2):
    x = x2.astype(F32)
    low = lax.broadcasted_iota(jnp.int32, x.shape, 1) < HEAD_DIM
    return [jnp.where(low, x, 0.0).astype(BF16), jnp.where(low, 0.0, x).astype(BF16)]


def _flash_kernel(q_ref, k_ref, vt_ref, o_ref, *scratch, tq, tk, moba):
    i = pl.program_id(2)
    nh = 2 * FLASH_PAIRS
    heads = range(nh)
    s_ref = [scratch[2 * hh:2 * hh + 2] for hh in heads]
    p_ref = [scratch[2 * nh + 2 * hh:2 * nh + 2 * hh + 2] for hh in heads]
    acc_ref = scratch[4 * nh:5 * nh]
    st_ref = scratch[5 * nh]
    mt_ref = scratch[5 * nh + 1]
    row = lambda r: slice(SUBLANES * r, SUBLANES * r + 1)
    moba_refs = scratch[5 * nh + 2:]
    dv = vt_ref.shape[1] // nh
    ones = jnp.ones((ONES_ROWS, tk), BF16)
    krow = lax.broadcasted_iota(jnp.int32, (tk, tq), 0)
    qcol = lax.broadcasted_iota(jnp.int32, (tk, tq), 1)

    if moba:
        kmean_ref, sel_ref = moba_refs
        nb = kmean_ref.shape[0]
        qs = []
        for pp in range(FLASH_PAIRS):
            qs += _split_heads(q_ref[0, :, pp * LANES:(pp + 1) * LANES])

        def ktile(start, hh):
            return k_ref[0, pl.ds(start, tk), (hh // 2) * LANES:(hh // 2 + 1) * LANES]

        @pl.when(i == 0)
        def _():
            for n in range(nb):
                rows = k_ref[0, n * tk:(n + 1) * tk, :].astype(F32)
                kmean_ref[n:n + 1, :] = jnp.sum(rows, axis=0, keepdims=True) * (1.0 / tk)

        km = kmean_ref[...]
        km_hi = km.astype(BF16)
        km_lo = (km - km_hi.astype(F32)).astype(BF16)
        pair_cols = lambda hh: slice((hh // 2) * LANES, (hh // 2 + 1) * LANES)
        blk = lax.broadcasted_iota(jnp.int32, (nb, tq), 0)
        lane = lax.broadcasted_iota(jnp.int32, (nb, tq), 1)
        qblk = i * (tq // tk)
        for t in range(1, tq // tk):
            qblk = qblk + jnp.where(lane >= t * tk, 1, 0)
        past = blk < qblk
        blk_f = blk.astype(F32)
        for hh in heads:
            gate = _dot_nt(km_hi[:, pair_cols(hh)], qs[hh]) + _dot_nt(km_lo[:, pair_cols(hh)], qs[hh])
            g = jnp.where(past, gate, NEG_INF)
            picked = jnp.zeros(gate.shape, F32)
            for _ in range(MOBA_TOPK):
                mx = jnp.max(g, axis=0, keepdims=True)
                first = jnp.min(jnp.where(g == mx, blk_f, float(nb)), axis=0, keepdims=True)
                hit = blk_f == first
                picked = jnp.where(hit, 1.0, picked)
                g = jnp.where(hit, -jnp.inf, g)
            sel_ref[hh] = jnp.where(past, picked, 0.0)
    else:
        dk = q_ref.shape[2] // nh
        qs = [q_ref[0, :, hh * dk:(hh + 1) * dk] for hh in heads]

        def ktile(start, hh):
            return k_ref[0, pl.ds(start, tk), hh * dk:(hh + 1) * dk]

    def mask(j, hh, s, off):
        causal = krow + off <= qcol
        if not moba:
            return jnp.where(causal, s, NEG_INF)
        same = jnp.where(jnp.logical_and(qcol >= off, qcol < off + tk), 1.0, 0.0)
        later = jnp.where(qcol >= off + tk, sel_ref[hh, pl.ds(j, 1), :], 0.0)
        allowed = jnp.where(same > 0.5, jnp.where(causal, 1.0, 0.0), later)
        return jnp.where(allowed > 0.5, s, NEG_INF)

    def put_scores(hh, slot, s):
        s_ref[hh][slot][...] = s
        mt_ref[row(2 * hh + slot), :] = jnp.max(s, axis=0, keepdims=True)

    def qk(j, hh):
        start = pl.multiple_of(j * tk, tk)
        return _dot_nt(ktile(start, hh), qs[hh])

    def pv(j, hh, p):
        start = pl.multiple_of(jnp.maximum(j, 0) * tk, tk)
        vt = jnp.concatenate([vt_ref[0, hh * dv:(hh + 1) * dv, pl.ds(start, tk)], ones], axis=0)
        return _dot(vt, p)

    def step(j, cur, off, last):
        pvrs = [pv(j - 1, 0, p_ref[0][cur][...])]
        if not last:
            for hh in heads:
                put_scores(hh, 1 - cur, qk(j + 1, hh))
        pvrs += [pv(j - 1, hh, p_ref[hh][cur][...]) for hh in heads[1:]]
        for hh in heads:
            m, alpha_prev = st_ref[row(2 * hh), :], st_ref[row(2 * hh + 1), :]
            if off is None:
                s = s_ref[hh][cur][...]
                tmax = mt_ref[row(2 * hh + cur), :]
                if moba:
                    chosen = sel_ref[hh, pl.ds(j, 1), :] > 0.5
                    m_new = jnp.maximum(m, jnp.where(chosen, tmax, NEG_INF))
                    shift = jnp.where(chosen, m_new, -NEG_INF)
                else:
                    m_new = jnp.maximum(m, tmax)
                    shift = m_new
            else:
                s = mask(j, hh, s_ref[hh][cur][...], off)
                m_new = jnp.maximum(m, jnp.max(s, axis=0, keepdims=True))
                shift = m_new
            alpha = jnp.exp2(m - m_new)
            p = jnp.exp2(s - shift)
            acc_ref[hh][...] = alpha_prev * acc_ref[hh][...] + pvrs[hh]
            p_ref[hh][1 - cur][...] = p.astype(BF16)
            st_ref[row(2 * hh), :] = m_new
            st_ref[row(2 * hh + 1), :] = alpha

    for hh in heads:
        put_scores(hh, 0, qk(0, hh))
        p_ref[hh][0][...] = jnp.zeros((tk, tq), BF16)
        acc_ref[hh][...] = jnp.zeros_like(acc_ref[hh])
        st_ref[row(2 * hh), :] = jnp.full((1, tq), NEG_INF, F32)
        st_ref[row(2 * hh + 1), :] = jnp.ones((1, tq), F32)

    def body(jj, carry):
        step(2 * jj, 0, None, False)
        step(2 * jj + 1, 1, None, False)
        return carry

    lax.fori_loop(0, i, body, 0)
    n_full = 2 * i
    step(n_full, 0, 0, False)
    step(n_full + 1, 1, tk, True)
    outs = []
    for hh in heads:
        acc = st_ref[row(2 * hh + 1), :] * acc_ref[hh][...] + pv(n_full + 1, hh, p_ref[hh][0][...])
        outs.append(acc[:dv] / acc[dv:dv + 1])
    o_ref[0] = jnp.concatenate(outs, axis=0).T.astype(o_ref.dtype)


def _flash_attention(q_arr, k_arr, vt, moba, q_blk0, k_blk0):
    b, s, _ = q_arr.shape
    dv = vt.shape[1] // N_HEADS
    nh = 2 * FLASH_PAIRS
    qw = nh * HEAD_DIM if moba else nh * MLA_QK_PAD
    tq, tk = FLASH_TQ, FLASH_TK
    assert tq == 2 * tk and s % tq == 0 and (not moba or tk == MOBA_BLOCK)
    scratch = ([pltpu.VMEM((tk, tq), F32)] * (2 * nh) + [pltpu.VMEM((tk, tq), BF16)] * (2 * nh)
               + [pltpu.VMEM((dv + ONES_ROWS, tq), F32)] * nh + [pltpu.VMEM((2 * nh * SUBLANES, tq), F32)] * 2)
    if moba:
        scratch += [pltpu.VMEM((s // tk, qw), F32), pltpu.VMEM((nh, s // tk, tq), F32)]
    return pl.pallas_call(
        functools.partial(_flash_kernel, tq=tq, tk=tk, moba=moba),
        grid=(b, N_HEADS // nh, s // tq),
        in_specs=[pl.BlockSpec((1, tq, qw), lambda b_, h_, i: (b_, i, q_blk0 + h_)),
                  pl.BlockSpec((1, s, qw), lambda b_, h_, i: (b_, 0, k_blk0 + h_)),
                  pl.BlockSpec((1, nh * dv, s), lambda b_, h_, i: (b_, h_, 0))],
        out_specs=pl.BlockSpec((1, tq, nh * dv), lambda b_, h_, i: (b_, i, h_)),
        out_shape=jax.ShapeDtypeStruct((b, s, N_HEADS * dv), BF16),
        scratch_shapes=scratch,
        compiler_params=_cparams(("parallel", "parallel", "arbitrary")),
        name="moba_attention" if moba else "mla_attention",
    )(q_arr, k_arr, vt)


def _sb_kernel(q_ref, k_ref, v_ref, o_ref, acc_ref, stay_ref, *, t):
    i = pl.program_id(2)
    w = 2 * HEAD_DIM
    heads = range(2 * SB_PAIRS)
    r = lax.broadcasted_iota(jnp.int32, (t, t), 0)
    c = lax.broadcasted_iota(jnp.int32, (t, t), 1)
    later = jnp.where(r > c, 1.0, 0.0).astype(BF16)
    past = c < r
    qs = []
    for pp in range(SB_PAIRS):
        qs += _split_heads(q_ref[0, :, pp * w:(pp + 1) * w] * 0.125)

    def tile(j, diag):
        start = pl.multiple_of(j * t, t)
        cols = [slice((hh // 2) * w, (hh // 2 + 1) * w) for hh in heads]
        zs = [_dot_nt(qs[hh], k_ref[0, pl.ds(start, t), cols[hh]]) for hh in heads]
        parts = []
        for z in zs:
            lp = jnp.log(1.0 + jnp.exp(-jnp.abs(z)))
            log_beta = jnp.minimum(z, 0.0) - lp
            log_stay = log_beta - z
            if diag:
                log_stay = jnp.where(past, log_stay, 0.0)
            hi = log_stay.astype(BF16)
            lo = (log_stay - hi.astype(F32)).astype(BF16)
            parts.append((log_beta, log_stay, hi, lo))
        inners = [_dot(hi, later) + _dot(lo, later) for _, _, hi, lo in parts]
        weights = []
        top = None
        for hh in heads:
            log_beta, log_stay, _, _ = parts[hh]
            a = jnp.exp(log_beta + inners[hh] + stay_ref[hh])
            if diag:
                a = jnp.where(past, a, 0.0)
            weights.append(a.astype(BF16))
            stay = stay_ref[hh] + jnp.sum(log_stay, axis=1, keepdims=True)
            stay_ref[hh] = stay
            mx = jnp.max(stay)
            top = mx if top is None else jnp.maximum(top, mx)
        for hh in heads:
            acc_ref[hh] += _dot(weights[hh], v_ref[0, pl.ds(start, t), cols[hh]])
        return top

    acc_ref[...] = jnp.zeros_like(acc_ref)
    stay_ref[...] = jnp.zeros_like(stay_ref)
    top = tile(i, True)

    def cond(state):
        j, top = state
        return jnp.logical_and(j >= 0, top > SB_UNDERFLOW)

    def body(state):
        j, _ = state
        return j - 1, tile(j, False)

    lax.while_loop(cond, body, (i - 1, top))
    low = lax.broadcasted_iota(jnp.int32, (t, w), 1) < HEAD_DIM
    for pp in range(SB_PAIRS):
        o_ref[0, :, pp * w:(pp + 1) * w] = jnp.where(low, acc_ref[2 * pp], acc_ref[2 * pp + 1]).astype(o_ref.dtype)


def _sb_attention(qkv, t=256):
    b, s, _ = qkv.shape
    w = 2 * HEAD_DIM * SB_PAIRS
    npair = N_HEADS // (2 * SB_PAIRS)
    return pl.pallas_call(
        functools.partial(_sb_kernel, t=t),
        grid=(b, npair, s // t),
        in_specs=[pl.BlockSpec((1, t, w), lambda b_, h_, i: (b_, i, h_)),
                  pl.BlockSpec((1, s, w), lambda b_, h_, i: (b_, 0, npair + h_)),
                  pl.BlockSpec((1, s, w), lambda b_, h_, i: (b_, 0, 2 * npair + h_))],
        out_specs=pl.BlockSpec((1, t, w), lambda b_, h_, i: (b_, i, h_)),
        out_shape=jax.ShapeDtypeStruct((b, s, N_HEADS * HEAD_DIM), BF16),
        scratch_shapes=[pltpu.VMEM((2 * SB_PAIRS, t, 2 * HEAD_DIM), F32), pltpu.VMEM((2 * SB_PAIRS, t, 1), F32)],
        compiler_params=_cparams(("parallel", "parallel", "arbitrary")),
        name="sb_attention",
    )(qkv, qkv, qkv)


def _dil_proj_kernel(x_ref, g_ref, w_ref, c_ref, sp_ref, sm_ref, o_ref, *scratch, dil, nrope, half):
    tm = x_ref.shape[0]
    xn = _rms(x_ref[...], g_ref[...]).astype(BF16)
    y = _dot(xn, w_ref[...])
    reps = nrope // LANES
    cc = jnp.tile(c_ref[...], (1, reps))
    sp = jnp.tile(sp_ref[...], (1, reps))
    sm = jnp.tile(sm_ref[...], (1, reps))
    yr = y[:, :nrope]
    yr = yr * cc + pltpu.roll(yr, half, axis=1) * sp + pltpu.roll(yr, nrope - half, axis=1) * sm
    if dil == 1:
        o_ref[0, 0, :, :nrope] = yr.astype(o_ref.dtype)
        o_ref[0, 0, :, nrope:] = y[:, nrope:].astype(o_ref.dtype)
    else:
        y_ref, = scratch
        for cb in range(y.shape[1] // LANES):
            cols = slice(cb * LANES, (cb + 1) * LANES)
            y_ref[cb] = yr[:, cols] if cb < nrope // LANES else y[:, cols]
            for r in range(dil):
                o_ref[0, r, :, cols] = y_ref[cb, pl.ds(r, tm // dil, stride=dil), :].astype(o_ref.dtype)


def _dil_proj(x, g, w, rope, b, seq, dil, tm=512):
    m, d = x.shape
    n = w.shape[1]
    nblk = seq // tm
    nrope = 2 * N_HEADS * HEAD_DIM
    in_specs = [pl.BlockSpec((tm, d), lambda i: (i, 0)), pl.BlockSpec((1, d), lambda i: (0, 0)),
                pl.BlockSpec((d, n), lambda i: (0, 0))]
    in_specs += [pl.BlockSpec((tm, LANES), lambda i: (i % nblk, 0))] * 3
    return pl.pallas_call(
        functools.partial(_dil_proj_kernel, dil=dil, nrope=nrope, half=HEAD_DIM // 2),
        grid=(m // tm,), in_specs=in_specs,
        out_specs=pl.BlockSpec((1, dil, tm // dil, n), lambda i: (i // nblk, 0, i % nblk, 0)),
        out_shape=jax.ShapeDtypeStruct((b, dil, seq // dil, n), BF16),
        scratch_shapes=[] if dil == 1 else [pltpu.VMEM((n // LANES, tm, LANES), F32)],
        compiler_params=_cparams(("parallel",)),
        name="dil_proj",
    )(x, g.reshape(1, d), w, *rope)


def _dil_kernel(q_ref, k_ref, v_ref, o_ref, *, nblk, unroll):
    t = DIL_BLOCK
    r = lax.broadcasted_iota(jnp.int32, (t, t), 0)
    c = lax.broadcasted_iota(jnp.int32, (t, t), 1)
    low = lax.broadcasted_iota(jnp.int32, (t, 2 * HEAD_DIM), 1) < HEAD_DIM

    def body(nn, _):
        scores, values = [], []
        for u in range(unroll):
            n = nn * unroll + u
            start = pl.multiple_of(n * t, t)
            prev = pl.multiple_of(jnp.maximum(n - 1, 0) * t, t)
            first = jnp.where(n > 0, 0, t)
            qs = _split_heads(q_ref[0, 0, pl.ds(start, t), :] * 0.125)
            k_cur, k_prev = k_ref[0, 0, pl.ds(start, t), :], k_ref[0, 0, pl.ds(prev, t), :]
            values.append((start, v_ref[0, 0, pl.ds(start, t), :], v_ref[0, 0, pl.ds(prev, t), :]))
            for hh in range(2):
                s_cur = jnp.where(c <= r, _dot_nt(qs[hh], k_cur), NEG_INF)
                s_prev = jnp.where(c >= r + first, _dot_nt(qs[hh], k_prev), NEG_INF)
                scores.append((s_cur, s_prev))
        probs = []
        for s_cur, s_prev in scores:
            m = jnp.max(jnp.maximum(s_cur, s_prev), axis=1, keepdims=True)
            p_cur = jnp.exp(s_cur - m)
            p_prev = jnp.exp(s_prev - m)
            l = jnp.sum(p_cur + p_prev, axis=1, keepdims=True)
            probs.append((p_cur.astype(BF16), p_prev.astype(BF16), l, m + jnp.log(l)))
        for u, (start, v_cur, v_prev) in enumerate(values):
            outs = []
            for hh in range(2):
                p_cur, p_prev, l, lse = probs[2 * u + hh]
                acc = _dot(p_cur, v_cur) + _dot(p_prev, v_prev)
                lse = jnp.broadcast_to(lse, acc.shape)
                outs.append(jnp.where(low, acc / l, lse) if hh == 0 else jnp.where(low, lse, acc / l))
            o_ref[0, 0, pl.ds(start, t), :] = jnp.concatenate(outs, axis=1)
        return 0

    lax.fori_loop(0, nblk // unroll, body, 0)


def _dil_attention(qkv):
    b, dil, length, _ = qkv.shape
    nblk = length // DIL_BLOCK
    w = 2 * HEAD_DIM
    npair = N_HEADS // 2

    def spec(first):
        return pl.BlockSpec((1, 1, length, w), lambda b_, r_, h_: (b_, r_, 0, first + h_))

    return pl.pallas_call(
        functools.partial(_dil_kernel, nblk=nblk, unroll=math.gcd(DIL_UNROLL, nblk)),
        grid=(b, dil, npair),
        in_specs=[spec(0), spec(npair), spec(2 * npair)],
        out_specs=pl.BlockSpec((1, 1, length, 2 * w), lambda b_, r_, h_: (b_, r_, 0, h_)),
        out_shape=jax.ShapeDtypeStruct((b, dil, length, N_HEADS * w), F32),
        compiler_params=_cparams(("parallel", "parallel", "parallel")),
        name="dil_attention",
    )(qkv, qkv, qkv)


def _dil_out_proj_kernel(a0_ref, a1_ref, a2_ref, w_ref, h_ref, o_ref, *scratch, dils):
    tm = h_ref.shape[0]
    xs = []
    scratch = list(scratch)
    for a_ref, dil in zip((a0_ref, a1_ref, a2_ref), dils):
        if dil == 1:
            xs.append(a_ref[0, 0])
        else:
            y_ref = scratch.pop(0)
            ncb = y_ref.shape[0]
            for cb in range(ncb):
                for r in range(dil):
                    y_ref[cb, pl.ds(r, tm // dil, stride=dil), :] = a_ref[0, r, :, cb * LANES:(cb + 1) * LANES]
            xs.append(jnp.concatenate([y_ref[cb] for cb in range(ncb)], axis=1))
    width = xs[0].shape[1]
    lane = lax.broadcasted_iota(jnp.int32, xs[0].shape, 1)
    low = lane % LANES < HEAD_DIM
    even = lane % (2 * LANES) < LANES
    mx = jnp.maximum(jnp.maximum(xs[0], xs[1]), xs[2])
    ws = [jnp.exp(x - mx) for x in xs]
    den = ws[0] + ws[1] + ws[2]
    comb = jnp.zeros_like(xs[0])
    for x, wgt in zip(xs, ws):
        wn = wgt / den
        wn = jnp.where(even, pltpu.roll(wn, width - HEAD_DIM, axis=1), pltpu.roll(wn, HEAD_DIM, axis=1))
        comb = comb + wn * x
    is_out = jnp.where(even, jnp.where(low, 1.0, 0.0), jnp.where(low, 0.0, 1.0))
    comb = jnp.where(is_out > 0.5, comb, 0.0).astype(BF16)
    o_ref[...] = h_ref[...] + _dot(comb, w_ref[...])


def _dil_out_proj(a_list, dils, w_exp, h, seq, tm=256):
    m, d = h.shape
    k = w_exp.shape[0]
    nblk = seq // tm
    in_specs = [pl.BlockSpec((1, dil, tm // dil, k), lambda i: (i // nblk, 0, i % nblk, 0)) for dil in dils]
    in_specs += [pl.BlockSpec((k, d), lambda i: (0, 0)), pl.BlockSpec((tm, d), lambda i: (i, 0))]
    return pl.pallas_call(
        functools.partial(_dil_out_proj_kernel, dils=dils), grid=(m // tm,),
        in_specs=in_specs,
        out_specs=pl.BlockSpec((tm, d), lambda i: (i, 0)),
        out_shape=jax.ShapeDtypeStruct((m, d), F32),
        scratch_shapes=[pltpu.VMEM((k // LANES, tm, LANES), F32) for dil in dils if dil > 1],
        compiler_params=_cparams(("parallel",)),
        name="dil_out_proj",
    )(*a_list, w_exp, h)


def _rope_tables(seq, dim, first_lane, period):
    half = dim // 2
    inv = ROPE_THETA ** (-jnp.arange(0, dim, 2, dtype=F32) / dim)
    ang = jnp.arange(seq, dtype=F32)[:, None] * inv[None, :]
    cos, sin = jnp.cos(ang), jnp.sin(ang)
    zeros = jnp.zeros_like(sin)
    pad_lo = jnp.zeros((seq, first_lane), F32)
    pad_hi = jnp.zeros((seq, period - first_lane - dim), F32)
    c = jnp.concatenate([pad_lo + 1.0, cos, cos, pad_hi + 1.0], axis=1)
    s_plus = jnp.concatenate([pad_lo, zeros, sin, pad_hi], axis=1)
    s_minus = jnp.concatenate([pad_lo, -sin, zeros, pad_hi], axis=1)
    reps = LANES // period
    return tuple(jnp.tile(t, (1, reps)) for t in (c, s_plus, s_minus))


def _mla_weights(w_in, w_uq, w_ukv):
    d = w_in.shape[0]
    ql, kvl = MLA_Q_LORA, MLA_KV_LORA
    w_kr = jnp.zeros((d, LANES), F32).at[:, MLA_NOPE:MLA_NOPE + MLA_ROPE].set(w_in[:, ql + kvl:])
    win = jnp.concatenate([w_in[:, :ql + kvl], w_kr], axis=1)
    wuq = w_uq.reshape(ql, N_HEADS, MLA_NOPE + MLA_ROPE)
    wuq = jnp.pad(wuq, ((0, 0), (0, 0), (0, MLA_QK_PAD - MLA_NOPE - MLA_ROPE))).reshape(ql, N_HEADS * MLA_QK_PAD)
    wukv = w_ukv.reshape(kvl, N_HEADS, MLA_NOPE + MLA_V)
    wuk = jnp.pad(wukv[:, :, :MLA_NOPE], ((0, 0), (0, 0), (0, MLA_QK_PAD - MLA_NOPE))).reshape(kvl, N_HEADS * MLA_QK_PAD)
    wuv = wukv[:, :, MLA_NOPE:].reshape(kvl, N_HEADS * MLA_V).T
    return win.astype(BF16), wuq.astype(BF16), wuk.astype(BF16), wuv.astype(BF16)


def kernel(x, norm_mix, norm_ffn, norm_final, a_w_qkv, a_w_o, b_w_in, b_q_norm, b_w_uq, b_kv_norm, b_w_ukv, b_w_o,
           c_w_qkv, c_w_o, d_w_qkv, d_w_o, ffn_w_gate, ffn_conv_w, ffn_conv_b, ffn_w_up, ffn_w_down):
    b, s, d = x.shape
    m = b * s
    hd = N_HEADS * HEAD_DIM
    depth = norm_mix.shape[0]
    rope_h = _rope_tables(s, HEAD_DIM, 0, HEAD_DIM)
    rope_r = _rope_tables(s, MLA_ROPE, MLA_NOPE, MLA_QK_PAD)
    h = x.reshape(m, d)
    for i in range(depth):
        kind, j = i % 4, i // 4
        g = norm_mix[i]
        if kind == 0:
            w = a_w_qkv[j].astype(BF16)
            qscale = jnp.concatenate([jnp.full((hd,), MOBA_Q_SCALE, F32), jnp.ones((hd,), F32)])
            qk = _norm_matmul(h, g, w[:, :2 * hd], s, rope=rope_h, half=HEAD_DIM // 2, colscale=qscale)
            vt = _norm_matmul_t(h, g, w[:, 2 * hd:].T, b, s)
            qk = qk.reshape(b, s, 2 * hd)
            o = _flash_attention(qk, qk, vt, True, 0, N_HEADS // (2 * FLASH_PAIRS))
            h = _out_proj(o.reshape(m, hd), a_w_o[j].astype(BF16), h)
        elif kind == 1:
            win, wuq, wuk, wuv = _mla_weights(b_w_in[j], b_w_uq[j], b_w_ukv[j])
            q, k, vt = _mla_proj(h, g, win, b_q_norm[j], b_kv_norm[j], wuq, wuk, wuv, rope_r, b, s)
            o = _flash_attention(q.reshape(b, s, -1), k.reshape(b, s, -1), vt, False, 0, 0)
            h = _out_proj(o.reshape(m, hd), b_w_o[j].astype(BF16), h)
        elif kind == 2:
            w = c_w_qkv[j].astype(BF16)
            dils = tuple(dil for _, dil in DIL_GROUPS)
            outs = [_dil_attention(_dil_proj(h, g, w[:, gi * 3 * hd:(gi + 1) * 3 * hd], rope_h, b, s, dil))
                    for gi, dil in enumerate(dils)]
            w_o = c_w_o[j].astype(BF16).reshape(N_HEADS // 2, 2, HEAD_DIM, d)
            zero = jnp.zeros_like(w_o[:, 0])
            w_o = jnp.concatenate([w_o[:, 0], zero, zero, w_o[:, 1]], axis=1).reshape(N_HEADS * 2 * HEAD_DIM, d)
            h = _dil_out_proj(outs, dils, w_o, h, s)
        else:
            qkv = _norm_matmul(h, g, d_w_qkv[j].astype(BF16), s).reshape(b, s, 3 * hd)
            h = _out_proj(_sb_attention(qkv).reshape(m, hd), d_w_o[j].astype(BF16), h)
        h = _ffn(h, norm_ffn[i], ffn_w_gate[i].astype(BF16), ffn_w_up[i].astype(BF16), ffn_w_down[i].astype(BF16),
                 ffn_conv_w[i], ffn_conv_b[i], s)
    return _final_norm(h, norm_final).reshape(b, s, d)
```

```python
import functools
import math

import jax
import jax.numpy as jnp
from jax import lax
from jax.experimental import pallas as pl
from jax.experimental.pallas import tpu as pltpu

D_MODEL = 1024
HEAD_DIM = 64
N_HEADS = 16
ROPE_THETA = 10000.0
RMS_EPS = 1e-6
NEG_INF = -1e30
MOBA_BLOCK = 256
MOBA_TOPK = 3
MLA_Q_LORA = 384
MLA_KV_LORA = 256
MLA_NOPE = 64
MLA_ROPE = 32
MLA_V = 64
MLA_QK_PAD = 128
DIL_GROUPS = ((128, 1), (512, 4), (2048, 16))
DIL_BLOCK = 128
D_FF = 2816
CONV_WIDTH = 3
LANES = 128
SUBLANES = 8
SB_UNDERFLOW = -104.0
VMEM_LIMIT = 56 * 1024 * 1024
LOG2E = 1.4426950408889634
MLA_Q_SCALE = (MLA_NOPE + MLA_ROPE) ** -0.5 * LOG2E
MOBA_Q_SCALE = HEAD_DIM ** -0.5 * LOG2E
FLASH_TQ = 512
FLASH_TK = 256
FLASH_PAIRS = 2
ONES_ROWS = 16
SB_PAIRS = 2
DIL_UNROLL = 4

F32 = jnp.float32
BF16 = jnp.bfloat16


def _cparams(sem):
    return pltpu.CompilerParams(dimension_semantics=sem, vmem_limit_bytes=VMEM_LIMIT)


def _dot(a, b):
    return jnp.dot(a, b, preferred_element_type=F32)


def _dot_nt(a, b):
    return lax.dot_general(a, b, (((1,), (1,)), ((), ())), preferred_element_type=F32)


def _rms(x, g):
    return x * lax.rsqrt(jnp.mean(x * x, axis=-1, keepdims=True) + RMS_EPS) * g


def _norm_matmul_kernel(*refs, rope, half, tn):
    if rope:
        x_ref, g_ref, w_ref, c_ref, sp_ref, sm_ref, scale_ref, o_ref = refs
    else:
        x_ref, g_ref, w_ref, o_ref = refs
    xn = _rms(x_ref[...], g_ref[...]).astype(BF16)
    y = _dot(xn, w_ref[...])
    if rope:
        reps = tn // LANES
        c = jnp.tile(c_ref[...], (1, reps))
        sp = jnp.tile(sp_ref[...], (1, reps))
        sm = jnp.tile(sm_ref[...], (1, reps))
        y = y * c + pltpu.roll(y, half, axis=1) * sp + pltpu.roll(y, tn - half, axis=1) * sm
        y = y * scale_ref[...]
    o_ref[...] = y.astype(o_ref.dtype)


def _norm_matmul(x, g, w, seq, *, rope=None, half=0, colscale=None, out_dtype=BF16, tm=512, tn=None):
    m, d = x.shape
    n = w.shape[1]
    tn = n if tn is None else tn
    grid = (n // tn, m // tm)
    in_specs = [pl.BlockSpec((tm, d), lambda j, i: (i, 0)),
                pl.BlockSpec((1, d), lambda j, i: (0, 0)),
                pl.BlockSpec((d, tn), lambda j, i: (0, j))]
    args = [x, g.reshape(1, d), w]
    if rope is not None:
        nblk = seq // tm
        for t in rope:
            in_specs.append(pl.BlockSpec((tm, LANES), lambda j, i: (i % nblk, 0)))
            args.append(t)
        in_specs.append(pl.BlockSpec((1, tn), lambda j, i: (0, j)))
        args.append(colscale.reshape(1, n))
    return pl.pallas_call(
        functools.partial(_norm_matmul_kernel, rope=rope is not None, half=half, tn=tn),
        grid=grid, in_specs=in_specs,
        out_specs=pl.BlockSpec((tm, tn), lambda j, i: (i, j)),
        out_shape=jax.ShapeDtypeStruct((m, n), out_dtype),
        compiler_params=_cparams(("parallel", "parallel")),
        name="norm_matmul",
    )(*args)


def _norm_matmul_t_kernel(x_ref, g_ref, wt_ref, o_ref):
    xn = _rms(x_ref[...], g_ref[...]).astype(BF16)
    o_ref[0] = _dot_nt(wt_ref[...], xn).astype(o_ref.dtype)


def _norm_matmul_t(x, g, wt, b, seq, tm=512):
    m, d = x.shape
    n = wt.shape[0]
    nblk = seq // tm
    return pl.pallas_call(
        _norm_matmul_t_kernel, grid=(m // tm,),
        in_specs=[pl.BlockSpec((tm, d), lambda i: (i, 0)), pl.BlockSpec((1, d), lambda i: (0, 0)),
                  pl.BlockSpec((n, d), lambda i: (0, 0))],
        out_specs=pl.BlockSpec((1, n, tm), lambda i: (i // nblk, 0, i % nblk)),
        out_shape=jax.ShapeDtypeStruct((b, n, seq), BF16),
        compiler_params=_cparams(("parallel",)),
        name="norm_matmul_t",
    )(x, g.reshape(1, d), wt)


def _mla_proj_kernel(x_ref, g_ref, win_ref, qn_ref, kvn_ref, wuq_ref, wuk_ref, wuv_ref,
                     c_ref, sp_ref, sm_ref, q_ref, k_ref, v_ref):
    half = MLA_ROPE // 2
    xn = _rms(x_ref[...], g_ref[...]).astype(BF16)
    c = _dot(xn, win_ref[...])
    cq = _rms(c[:, :MLA_Q_LORA], qn_ref[...]).astype(BF16)
    ckv = _rms(c[:, MLA_Q_LORA:MLA_Q_LORA + MLA_KV_LORA], kvn_ref[...]).astype(BF16)
    kr = c[:, MLA_Q_LORA + MLA_KV_LORA:]

    def rot(y, width):
        reps = width // LANES
        cc = jnp.tile(c_ref[...], (1, reps))
        sp = jnp.tile(sp_ref[...], (1, reps))
        sm = jnp.tile(sm_ref[...], (1, reps))
        return y * cc + pltpu.roll(y, half, axis=1) * sp + pltpu.roll(y, width - half, axis=1) * sm

    nq = wuq_ref.shape[1]
    q = rot(_dot(cq, wuq_ref[...]), nq) * MLA_Q_SCALE
    q_ref[...] = q.astype(q_ref.dtype)
    kr = rot(kr, LANES)
    k = _dot(ckv, wuk_ref[...]) + jnp.tile(kr, (1, nq // LANES))
    k_ref[...] = k.astype(k_ref.dtype)
    v_ref[0] = _dot_nt(wuv_ref[...], ckv).astype(v_ref.dtype)


def _mla_proj(x, g, win, qn, kvn, wuq, wuk, wuv, rope, b, seq, tm=512):
    m, d = x.shape
    nblk = seq // tm
    full = lambda a: pl.BlockSpec(a.shape, lambda i: (0, 0))
    qn2, kvn2, g2 = qn.reshape(1, -1), kvn.reshape(1, -1), g.reshape(1, -1)
    in_specs = [pl.BlockSpec((tm, d), lambda i: (i, 0)), full(g2), full(win), full(qn2), full(kvn2),
                full(wuq), full(wuk), full(wuv)]
    in_specs += [pl.BlockSpec((tm, LANES), lambda i: (i % nblk, 0))] * 3
    nq, nv = wuq.shape[1], wuv.shape[0]
    return pl.pallas_call(
        _mla_proj_kernel, grid=(m // tm,), in_specs=in_specs,
        out_specs=[pl.BlockSpec((tm, nq), lambda i: (i, 0)), pl.BlockSpec((tm, nq), lambda i: (i, 0)),
                   pl.BlockSpec((1, nv, tm), lambda i: (i // nblk, 0, i % nblk))],
        out_shape=[jax.ShapeDtypeStruct((m, nq), BF16), jax.ShapeDtypeStruct((m, nq), BF16),
                   jax.ShapeDtypeStruct((b, nv, seq), BF16)],
        compiler_params=_cparams(("parallel",)),
        name="mla_proj",
    )(x, g2, win, qn2, kvn2, wuq, wuk, wuv, *rope)


def _out_proj_kernel(a_ref, w_ref, h_ref, o_ref):
    o_ref[...] = h_ref[...] + _dot(a_ref[...], w_ref[...])


def _out_proj(a, w, h, tm=512):
    m, k = a.shape
    d = w.shape[1]
    return pl.pallas_call(
        _out_proj_kernel, grid=(m // tm,),
        in_specs=[pl.BlockSpec((tm, k), lambda i: (i, 0)), pl.BlockSpec((k, d), lambda i: (0, 0)),
                  pl.BlockSpec((tm, d), lambda i: (i, 0))],
        out_specs=pl.BlockSpec((tm, d), lambda i: (i, 0)),
        out_shape=jax.ShapeDtypeStruct((m, d), F32),
        compiler_params=_cparams(("parallel",)),
        name="out_proj",
    )(a, w, h)


def _gelu_exact(x):
    return 0.5 * x * (1.0 + lax.erf(x * (1.0 / math.sqrt(2.0))))


def _ffn_kernel(x_ref, g_ref, wg_ref, wu_ref, wd_ref, cw_ref, cb_ref, gf_ref, o_ref, xn_ref, acc_ref, carry_ref,
                *, nblk, fc, final_norm):
    i, c = pl.program_id(0), pl.program_id(1)
    nc = pl.num_programs(1)
    tm = x_ref.shape[0]

    @pl.when(c == 0)
    def _():
        xn_ref[...] = _rms(x_ref[...], g_ref[...]).astype(BF16)
        acc_ref[...] = jnp.zeros_like(acc_ref)

    @pl.when(jnp.logical_and(c == 0, i % nblk == 0))
    def _():
        carry_ref[...] = jnp.zeros_like(carry_ref)

    xn = xn_ref[...]
    gate = _dot(xn, wg_ref[...])
    up = _dot(xn, wu_ref[...])
    col = pl.multiple_of(c * fc, LANES)
    prev = carry_ref[:, pl.ds(col, fc)]
    row = lax.broadcasted_iota(jnp.int32, gate.shape, 0)
    p1, p2 = prev[7:8, :], prev[6:7, :]
    s1 = jnp.where(row == 0, p1, pltpu.roll(gate, 1, axis=0))
    s2 = jnp.where(row == 0, p2, jnp.where(row == 1, p1, pltpu.roll(gate, 2, axis=0)))
    carry_ref[:, pl.ds(col, fc)] = gate[tm - 8:, :]
    cw = cw_ref[0]
    gc = cb_ref[0] + cw[0:1, :] * s2 + cw[1:2, :] * s1 + cw[2:3, :] * gate
    hidden = (_gelu_exact(gc) * up).astype(BF16)
    acc_ref[...] += _dot(hidden, wd_ref[...])

    @pl.when(c == nc - 1)
    def _():
        y = x_ref[...] + acc_ref[...]
        o_ref[...] = _rms(y, gf_ref[...]) if final_norm else y


def _ffn(h, g, wg, wu, wd, cw, cb, gf, final_norm, seq, tm=512, fc=1408):
    m, d = h.shape
    f = wg.shape[1]
    nc = f // fc
    cw3 = cw.reshape(CONV_WIDTH, nc, fc).transpose(1, 0, 2)
    cb3 = cb.reshape(nc, 1, fc)
    return pl.pallas_call(
        functools.partial(_ffn_kernel, nblk=seq // tm, fc=fc, final_norm=final_norm),
        grid=(m // tm, nc),
        in_specs=[pl.BlockSpec((tm, d), lambda i, c: (i, 0)),
                  pl.BlockSpec((1, d), lambda i, c: (0, 0)),
                  pl.BlockSpec((d, fc), lambda i, c: (0, c)),
                  pl.BlockSpec((d, fc), lambda i, c: (0, c)),
                  pl.BlockSpec((fc, d), lambda i, c: (c, 0)),
                  pl.BlockSpec((1, CONV_WIDTH, fc), lambda i, c: (c, 0, 0)),
                  pl.BlockSpec((1, 1, fc), lambda i, c: (c, 0, 0)),
                  pl.BlockSpec((1, d), lambda i, c: (0, 0))],
        out_specs=pl.BlockSpec((tm, d), lambda i, c: (i, 0)),
        out_shape=jax.ShapeDtypeStruct((m, d), F32),
        scratch_shapes=[pltpu.VMEM((tm, d), BF16), pltpu.VMEM((tm, d), F32), pltpu.VMEM((8, f), F32)],
        compiler_params=_cparams(("arbitrary", "arbitrary")),
        name="conv_ffn",
    )(h, g.reshape(1, d), wg, wu, wd, cw3, cb3, gf.reshape(1, d))


def _final_norm_kernel(x_ref, g_ref, o_ref):
    o_ref[...] = _rms(x_ref[...], g_ref[...])


def _final_norm(h, g, tm=1024):
    m, d = h.shape
    return pl.pallas_call(
        _final_norm_kernel, grid=(m // tm,),
        in_specs=[pl.BlockSpec((tm, d), lambda i: (i, 0)), pl.BlockSpec((1, d), lambda i: (0, 0))],
        out_specs=pl.BlockSpec((tm, d), lambda i: (i, 0)),
        out_shape=jax.ShapeDtypeStruct((m, d), F32),
        compiler_params=_cparams(("parallel",)),
        name="final_norm",
    )(h, g.reshape(1, d))


def _split_heads(x2):
    x = x2.astype(F32)
    low = lax.broadcasted_iota(jnp.int32, x.shape, 1) < HEAD_DIM
    return [jnp.where(low, x, 0.0).astype(BF16), jnp.where(low, 0.0, x).astype(BF16)]


def _flash_kernel(q_ref, k_ref, vt_ref, o_ref, *scratch, tq, tk, moba):
    i = pl.program_id(2)
    nh = 2 * FLASH_PAIRS
    heads = range(nh)
    s_ref = [scratch[2 * hh:2 * hh + 2] for hh in heads]
    p_ref = [scratch[2 * nh + 2 * hh:2 * nh + 2 * hh + 2] for hh in heads]
    acc_ref = scratch[4 * nh:5 * nh]
    st_ref = scratch[5 * nh]
    mt_ref = scratch[5 * nh + 1]
    row = lambda r: slice(SUBLANES * r, SUBLANES * r + 1)
    moba_refs = scratch[5 * nh + 2:]
    dv = vt_ref.shape[1] // nh
    ones = jnp.ones((ONES_ROWS, tk), BF16)
    krow = lax.broadcasted_iota(jnp.int32, (tk, tq), 0)
    qcol = lax.broadcasted_iota(jnp.int32, (tk, tq), 1)

    if moba:
        kmean_ref, sel_ref = moba_refs
        nb = kmean_ref.shape[0]
        qs = []
        for pp in range(FLASH_PAIRS):
            qs += _split_heads(q_ref[0, :, pp * LANES:(pp + 1) * LANES])

        def ktile(start, hh):
            return k_ref[0, pl.ds(start, tk), (hh // 2) * LANES:(hh // 2 + 1) * LANES]

        @pl.when(i == 0)
        def _():
            for n in range(nb):
                rows = k_ref[0, n * tk:(n + 1) * tk, :].astype(F32)
                kmean_ref[n:n + 1, :] = jnp.sum(rows, axis=0, keepdims=True) * (1.0 / tk)

        km = kmean_ref[...]
        km_hi = km.astype(BF16)
        km_lo = (km - km_hi.astype(F32)).astype(BF16)
        pair_cols = lambda hh: slice((hh // 2) * LANES, (hh // 2 + 1) * LANES)
        blk = lax.broadcasted_iota(jnp.int32, (nb, tq), 0)
        lane = lax.broadcasted_iota(jnp.int32, (nb, tq), 1)
        qblk = i * (tq // tk)
        for t in range(1, tq // tk):
            qblk = qblk + jnp.where(lane >= t * tk, 1, 0)
        past = blk < qblk
        blk_f = blk.astype(F32)
        for hh in heads:
            gate = _dot_nt(km_hi[:, pair_cols(hh)], qs[hh]) + _dot_nt(km_lo[:, pair_cols(hh)], qs[hh])
            g = jnp.where(past, gate, NEG_INF)
            picked = jnp.zeros(gate.shape, F32)
            for _ in range(MOBA_TOPK):
                mx = jnp.max(g, axis=0, keepdims=True)
                first = jnp.min(jnp.where(g == mx, blk_f, float(nb)), axis=0, keepdims=True)
                hit = blk_f == first
                picked = jnp.where(hit, 1.0, picked)
                g = jnp.where(hit, -jnp.inf, g)
            sel_ref[hh] = jnp.where(past, picked, 0.0)
    else:
        dk = q_ref.shape[2] // nh
        qs = [q_ref[0, :, hh * dk:(hh + 1) * dk] for hh in heads]

        def ktile(start, hh):
            return k_ref[0, pl.ds(start, tk), hh * dk:(hh + 1) * dk]

    def mask(j, hh, s, off):
        causal = krow + off <= qcol
        if not moba:
            return jnp.where(causal, s, NEG_INF)
        same = jnp.where(jnp.logical_and(qcol >= off, qcol < off + tk), 1.0, 0.0)
        later = jnp.where(qcol >= off + tk, sel_ref[hh, pl.ds(j, 1), :], 0.0)
        allowed = jnp.where(same > 0.5, jnp.where(causal, 1.0, 0.0), later)
        return jnp.where(allowed > 0.5, s, NEG_INF)

    def put_scores(hh, slot, s):
        s_ref[hh][slot][...] = s
        mt_ref[row(2 * hh + slot), :] = jnp.max(s, axis=0, keepdims=True)

    def qk(j, hh):
        start = pl.multiple_of(j * tk, tk)
        return _dot_nt(ktile(start, hh), qs[hh])

    def pv(j, hh, p):
        start = pl.multiple_of(jnp.maximum(j, 0) * tk, tk)
        vt = jnp.concatenate([vt_ref[0, hh * dv:(hh + 1) * dv, pl.ds(start, tk)], ones], axis=0)
        return _dot(vt, p)

    def step(j, cur, off, last):
        pvrs = [pv(j - 1, 0, p_ref[0][cur][...])]
        if not last:
            for hh in heads:
                put_scores(hh, 1 - cur, qk(j + 1, hh))
        pvrs += [pv(j - 1, hh, p_ref[hh][cur][...]) for hh in heads[1:]]
        for hh in heads:
            m, alpha_prev = st_ref[row(2 * hh), :], st_ref[row(2 * hh + 1), :]
            if off is None:
                s = s_ref[hh][cur][...]
                tmax = mt_ref[row(2 * hh + cur), :]
                if moba:
                    chosen = sel_ref[hh, pl.ds(j, 1), :] > 0.5
                    m_new = jnp.maximum(m, jnp.where(chosen, tmax, NEG_INF))
                    shift = jnp.where(chosen, m_new, -NEG_INF)
                else:
                    m_new = jnp.maximum(m, tmax)
                    shift = m_new
            else:
                s = mask(j, hh, s_ref[hh][cur][...], off)
                m_new = jnp.maximum(m, jnp.max(s, axis=0, keepdims=True))
                shift = m_new
            alpha = jnp.exp2(m - m_new)
            p = jnp.exp2(s - shift)
            acc_ref[hh][...] = alpha_prev * acc_ref[hh][...] + pvrs[hh]
            p_ref[hh][1 - cur][...] = p.astype(BF16)
            st_ref[row(2 * hh), :] = m_new
            st_ref[row(2 * hh + 1), :] = alpha

    for hh in heads:
        put_scores(hh, 0, qk(0, hh))
        p_ref[hh][0][...] = jnp.zeros((tk, tq), BF16)
        acc_ref[hh][...] = jnp.zeros_like(acc_ref[hh])
        st_ref[row(2 * hh), :] = jnp.full((1, tq), NEG_INF, F32)
        st_ref[row(2 * hh + 1), :] = jnp.ones((1, tq), F32)

    def body(jj, carry):
        step(2 * jj, 0, None, False)
        step(2 * jj + 1, 1, None, False)
        return carry

    lax.fori_loop(0, i, body, 0)
    n_full = 2 * i
    step(n_full, 0, 0, False)
    step(n_full + 1, 1, tk, True)
    outs = []
    for hh in heads:
        acc = st_ref[row(2 * hh + 1), :] * acc_ref[hh][...] + pv(n_full + 1, hh, p_ref[hh][0][...])
        outs.append(acc[:dv] / acc[dv:dv + 1])
    o_ref[0] = jnp.concatenate(outs, axis=0).T.astype(o_ref.dtype)


def _flash_attention(q_arr, k_arr, vt, moba, q_blk0, k_blk0):
    b, s, _ = q_arr.shape
    dv = vt.shape[1] // N_HEADS
    nh = 2 * FLASH_PAIRS
    qw = nh * HEAD_DIM if moba else nh * MLA_QK_PAD
    tq, tk = FLASH_TQ, FLASH_TK
    assert tq == 2 * tk and s % tq == 0 and (not moba or tk == MOBA_BLOCK)
    scratch = ([pltpu.VMEM((tk, tq), F32)] * (2 * nh) + [pltpu.VMEM((tk, tq), BF16)] * (2 * nh)
               + [pltpu.VMEM((dv + ONES_ROWS, tq), F32)] * nh + [pltpu.VMEM((2 * nh * SUBLANES, tq), F32)] * 2)
    if moba:
        scratch += [pltpu.VMEM((s // tk, qw), F32), pltpu.VMEM((nh, s // tk, tq), F32)]
    return pl.pallas_call(
        functools.partial(_flash_kernel, tq=tq, tk=tk, moba=moba),
        grid=(b, N_HEADS // nh, s // tq),
        in_specs=[pl.BlockSpec((1, tq, qw), lambda b_, h_, i: (b_, i, q_blk0 + h_)),
                  pl.BlockSpec((1, s, qw), lambda b_, h_, i: (b_, 0, k_blk0 + h_)),
                  pl.BlockSpec((1, nh * dv, s), lambda b_, h_, i: (b_, h_, 0))],
        out_specs=pl.BlockSpec((1, tq, nh * dv), lambda b_, h_, i: (b_, i, h_)),
        out_shape=jax.ShapeDtypeStruct((b, s, N_HEADS * dv), BF16),
        scratch_shapes=scratch,
        compiler_params=_cparams(("parallel", "parallel", "arbitrary")),
        name="moba_attention" if moba else "mla_attention",
    )(q_arr, k_arr, vt)


def _sb_kernel(q_ref, k_ref, v_ref, o_ref, acc_ref, stay_ref, *, t):
    i = pl.program_id(2)
    w = 2 * HEAD_DIM
    heads = range(2 * SB_PAIRS)
    r = lax.broadcasted_iota(jnp.int32, (t, t), 0)
    c = lax.broadcasted_iota(jnp.int32, (t, t), 1)
    later = jnp.where(r > c, 1.0, 0.0).astype(BF16)
    past = c < r
    qs = []
    for pp in range(SB_PAIRS):
        qs += _split_heads(q_ref[0, :, pp * w:(pp + 1) * w] * 0.125)

    def tile(j, diag):
        start = pl.multiple_of(j * t, t)
        cols = [slice((hh // 2) * w, (hh // 2 + 1) * w) for hh in heads]
        zs = [_dot_nt(qs[hh], k_ref[0, pl.ds(start, t), cols[hh]]) for hh in heads]
        parts = []
        for z in zs:
            lp = jnp.log(1.0 + jnp.exp(-jnp.abs(z)))
            log_beta = jnp.minimum(z, 0.0) - lp
            log_stay = log_beta - z
            if diag:
                log_stay = jnp.where(past, log_stay, 0.0)
            hi = log_stay.astype(BF16)
            lo = (log_stay - hi.astype(F32)).astype(BF16)
            parts.append((log_beta, log_stay, hi, lo))
        inners = [_dot(hi, later) + _dot(lo, later) for _, _, hi, lo in parts]
        weights = []
        top = None
        for hh in heads:
            log_beta, log_stay, _, _ = parts[hh]
            a = jnp.exp(log_beta + inners[hh] + stay_ref[hh])
            if diag:
                a = jnp.where(past, a, 0.0)
            weights.append(a.astype(BF16))
            stay = stay_ref[hh] + jnp.sum(log_stay, axis=1, keepdims=True)
            stay_ref[hh] = stay
            mx = jnp.max(stay)
            top = mx if top is None else jnp.maximum(top, mx)
        for hh in heads:
            acc_ref[hh] += _dot(weights[hh], v_ref[0, pl.ds(start, t), cols[hh]])
        return top

    acc_ref[...] = jnp.zeros_like(acc_ref)
    stay_ref[...] = jnp.zeros_like(stay_ref)
    top = tile(i, True)

    def cond(state):
        j, top = state
        return jnp.logical_and(j >= 0, top > SB_UNDERFLOW)

    def body(state):
        j, _ = state
        return j - 1, tile(j, False)

    lax.while_loop(cond, body, (i - 1, top))
    low = lax.broadcasted_iota(jnp.int32, (t, w), 1) < HEAD_DIM
    for pp in range(SB_PAIRS):
        o_ref[0, :, pp * w:(pp + 1) * w] = jnp.where(low, acc_ref[2 * pp], acc_ref[2 * pp + 1]).astype(o_ref.dtype)


def _sb_attention(qkv, t=256):
    b, s, _ = qkv.shape
    w = 2 * HEAD_DIM * SB_PAIRS
    npair = N_HEADS // (2 * SB_PAIRS)
    return pl.pallas_call(
        functools.partial(_sb_kernel, t=t),
        grid=(b, npair, s // t),
        in_specs=[pl.BlockSpec((1, t, w), lambda b_, h_, i: (b_, i, h_)),
                  pl.BlockSpec((1, s, w), lambda b_, h_, i: (b_, 0, npair + h_)),
                  pl.BlockSpec((1, s, w), lambda b_, h_, i: (b_, 0, 2 * npair + h_))],
        out_specs=pl.BlockSpec((1, t, w), lambda b_, h_, i: (b_, i, h_)),
        out_shape=jax.ShapeDtypeStruct((b, s, N_HEADS * HEAD_DIM), BF16),
        scratch_shapes=[pltpu.VMEM((2 * SB_PAIRS, t, 2 * HEAD_DIM), F32), pltpu.VMEM((2 * SB_PAIRS, t, 1), F32)],
        compiler_params=_cparams(("parallel", "parallel", "arbitrary")),
        name="sb_attention",
    )(qkv, qkv, qkv)


def _dil_proj_kernel(x_ref, g_ref, w_ref, c_ref, sp_ref, sm_ref, o_ref, *scratch, dil, nrope, half):
    tm = x_ref.shape[0]
    xn = _rms(x_ref[...], g_ref[...]).astype(BF16)
    y = _dot(xn, w_ref[...])
    reps = nrope // LANES
    cc = jnp.tile(c_ref[...], (1, reps))
    sp = jnp.tile(sp_ref[...], (1, reps))
    sm = jnp.tile(sm_ref[...], (1, reps))
    yr = y[:, :nrope]
    yr = yr * cc + pltpu.roll(yr, half, axis=1) * sp + pltpu.roll(yr, nrope - half, axis=1) * sm
    if dil == 1:
        o_ref[0, 0, :, :nrope] = yr.astype(o_ref.dtype)
        o_ref[0, 0, :, nrope:] = y[:, nrope:].astype(o_ref.dtype)
    else:
        y_ref, = scratch
        for cb in range(y.shape[1] // LANES):
            cols = slice(cb * LANES, (cb + 1) * LANES)
            y_ref[cb] = yr[:, cols] if cb < nrope // LANES else y[:, cols]
            for r in range(dil):
                o_ref[0, r, :, cols] = y_ref[cb, pl.ds(r, tm // dil, stride=dil), :].astype(o_ref.dtype)


def _dil_proj(x, g, w, rope, b, seq, dil, tm=512):
    m, d = x.shape
    n = w.shape[1]
    nblk = seq // tm
    nrope = 2 * N_HEADS * HEAD_DIM
    in_specs = [pl.BlockSpec((tm, d), lambda i: (i, 0)), pl.BlockSpec((1, d), lambda i: (0, 0)),
                pl.BlockSpec((d, n), lambda i: (0, 0))]
    in_specs += [pl.BlockSpec((tm, LANES), lambda i: (i % nblk, 0))] * 3
    return pl.pallas_call(
        functools.partial(_dil_proj_kernel, dil=dil, nrope=nrope, half=HEAD_DIM // 2),
        grid=(m // tm,), in_specs=in_specs,
        out_specs=pl.BlockSpec((1, dil, tm // dil, n), lambda i: (i // nblk, 0, i % nblk, 0)),
        out_shape=jax.ShapeDtypeStruct((b, dil, seq // dil, n), BF16),
        scratch_shapes=[] if dil == 1 else [pltpu.VMEM((n // LANES, tm, LANES), F32)],
        compiler_params=_cparams(("parallel",)),
        name="dil_proj",
    )(x, g.reshape(1, d), w, *rope)


def _dil_kernel(q_ref, k_ref, v_ref, o_ref, *, nblk, unroll):
    t = DIL_BLOCK
    r = lax.broadcasted_iota(jnp.int32, (t, t), 0)
    c = lax.broadcasted_iota(jnp.int32, (t, t), 1)
    low = lax.broadcasted_iota(jnp.int32, (t, 2 * HEAD_DIM), 1) < HEAD_DIM

    def body(nn, _):
        scores, values = [], []
        for u in range(unroll):
            n = nn * unroll + u
            start = pl.multiple_of(n * t, t)
            prev = pl.multiple_of(jnp.maximum(n - 1, 0) * t, t)
            first = jnp.where(n > 0, 0, t)
            qs = _split_heads(q_ref[0, 0, pl.ds(start, t), :] * 0.125)
            k_cur, k_prev = k_ref[0, 0, pl.ds(start, t), :], k_ref[0, 0, pl.ds(prev, t), :]
            values.append((start, v_ref[0, 0, pl.ds(start, t), :], v_ref[0, 0, pl.ds(prev, t), :]))
            for hh in range(2):
                s_cur = jnp.where(c <= r, _dot_nt(qs[hh], k_cur), NEG_INF)
                s_prev = jnp.where(c >= r + first, _dot_nt(qs[hh], k_prev), NEG_INF)
                scores.append((s_cur, s_prev))
        probs = []
        for s_cur, s_prev in scores:
            m = jnp.max(jnp.maximum(s_cur, s_prev), axis=1, keepdims=True)
            p_cur = jnp.exp(s_cur - m)
            p_prev = jnp.exp(s_prev - m)
            l = jnp.sum(p_cur + p_prev, axis=1, keepdims=True)
            probs.append((p_cur.astype(BF16), p_prev.astype(BF16), l, m + jnp.log(l)))
        for u, (start, v_cur, v_prev) in enumerate(values):
            outs = []
            for hh in range(2):
                p_cur, p_prev, l, lse = probs[2 * u + hh]
                acc = _dot(p_cur, v_cur) + _dot(p_prev, v_prev)
                lse = jnp.broadcast_to(lse, acc.shape)
                outs.append(jnp.where(low, acc / l, lse) if hh == 0 else jnp.where(low, lse, acc / l))
            o_ref[0, 0, pl.ds(start, t), :] = jnp.concatenate(outs, axis=1)
        return 0

    lax.fori_loop(0, nblk // unroll, body, 0)


def _dil_attention(qkv):
    b, dil, length, _ = qkv.shape
    nblk = length // DIL_BLOCK
    w = 2 * HEAD_DIM
    npair = N_HEADS // 2

    def spec(first):
        return pl.BlockSpec((1, 1, length, w), lambda b_, r_, h_: (b_, r_, 0, first + h_))

    return pl.pallas_call(
        functools.partial(_dil_kernel, nblk=nblk, unroll=math.gcd(DIL_UNROLL, nblk)),
        grid=(b, dil, npair),
        in_specs=[spec(0), spec(npair), spec(2 * npair)],
        out_specs=pl.BlockSpec((1, 1, length, 2 * w), lambda b_, r_, h_: (b_, r_, 0, h_)),
        out_shape=jax.ShapeDtypeStruct((b, dil, length, N_HEADS * w), F32),
        compiler_params=_cparams(("parallel", "parallel", "parallel")),
        name="dil_attention",
    )(qkv, qkv, qkv)


def _dil_out_proj_kernel(a0_ref, a1_ref, a2_ref, w_ref, h_ref, o_ref, *scratch, dils):
    tm = h_ref.shape[0]
    xs = []
    scratch = list(scratch)
    for a_ref, dil in zip((a0_ref, a1_ref, a2_ref), dils):
        if dil == 1:
            xs.append(a_ref[0, 0])
        else:
            y_ref = scratch.pop(0)
            ncb = y_ref.shape[0]
            for cb in range(ncb):
                for r in range(dil):
                    y_ref[cb, pl.ds(r, tm // dil, stride=dil), :] = a_ref[0, r, :, cb * LANES:(cb + 1) * LANES]
            xs.append(jnp.concatenate([y_ref[cb] for cb in range(ncb)], axis=1))
    width = xs[0].shape[1]
    lane = lax.broadcasted_iota(jnp.int32, xs[0].shape, 1)
    low = lane % LANES < HEAD_DIM
    even = lane % (2 * LANES) < LANES
    mx = jnp.maximum(jnp.maximum(xs[0], xs[1]), xs[2])
    ws = [jnp.exp(x - mx) for x in xs]
    den = ws[0] + ws[1] + ws[2]
    comb = jnp.zeros_like(xs[0])
    for x, wgt in zip(xs, ws):
        wn = wgt / den
        wn = jnp.where(even, pltpu.roll(wn, width - HEAD_DIM, axis=1), pltpu.roll(wn, HEAD_DIM, axis=1))
        comb = comb + wn * x
    is_out = jnp.where(even, jnp.where(low, 1.0, 0.0), jnp.where(low, 0.0, 1.0))
    comb = jnp.where(is_out > 0.5, comb, 0.0).astype(BF16)
    o_ref[...] = h_ref[...] + _dot(comb, w_ref[...])


def _dil_out_proj(a_list, dils, w_exp, h, seq, tm=256):
    m, d = h.shape
    k = w_exp.shape[0]
    nblk = seq // tm
    in_specs = [pl.BlockSpec((1, dil, tm // dil, k), lambda i: (i // nblk, 0, i % nblk, 0)) for dil in dils]
    in_specs += [pl.BlockSpec((k, d), lambda i: (0, 0)), pl.BlockSpec((tm, d), lambda i: (i, 0))]
    return pl.pallas_call(
        functools.partial(_dil_out_proj_kernel, dils=dils), grid=(m // tm,),
        in_specs=in_specs,
        out_specs=pl.BlockSpec((tm, d), lambda i: (i, 0)),
        out_shape=jax.ShapeDtypeStruct((m, d), F32),
        scratch_shapes=[pltpu.VMEM((k // LANES, tm, LANES), F32) for dil in dils if dil > 1],
        compiler_params=_cparams(("parallel",)),
        name="dil_out_proj",
    )(*a_list, w_exp, h)


def _rope_tables(seq, dim, first_lane, period):
    half = dim // 2
    inv = ROPE_THETA ** (-jnp.arange(0, dim, 2, dtype=F32) / dim)
    ang = jnp.arange(seq, dtype=F32)[:, None] * inv[None, :]
    cos, sin = jnp.cos(ang), jnp.sin(ang)
    zeros = jnp.zeros_like(sin)
    pad_lo = jnp.zeros((seq, first_lane), F32)
    pad_hi = jnp.zeros((seq, period - first_lane - dim), F32)
    c = jnp.concatenate([pad_lo + 1.0, cos, cos, pad_hi + 1.0], axis=1)
    s_plus = jnp.concatenate([pad_lo, zeros, sin, pad_hi], axis=1)
    s_minus = jnp.concatenate([pad_lo, -sin, zeros, pad_hi], axis=1)
    reps = LANES // period
    return tuple(jnp.tile(t, (1, reps)) for t in (c, s_plus, s_minus))


def _mla_weights(w_in, w_uq, w_ukv):
    d = w_in.shape[0]
    ql, kvl = MLA_Q_LORA, MLA_KV_LORA
    w_kr = jnp.zeros((d, LANES), F32).at[:, MLA_NOPE:MLA_NOPE + MLA_ROPE].set(w_in[:, ql + kvl:])
    win = jnp.concatenate([w_in[:, :ql + kvl], w_kr], axis=1)
    wuq = w_uq.reshape(ql, N_HEADS, MLA_NOPE + MLA_ROPE)
    wuq = jnp.pad(wuq, ((0, 0), (0, 0), (0, MLA_QK_PAD - MLA_NOPE - MLA_ROPE))).reshape(ql, N_HEADS * MLA_QK_PAD)
    wukv = w_ukv.reshape(kvl, N_HEADS, MLA_NOPE + MLA_V)
    wuk = jnp.pad(wukv[:, :, :MLA_NOPE], ((0, 0), (0, 0), (0, MLA_QK_PAD - MLA_NOPE))).reshape(kvl, N_HEADS * MLA_QK_PAD)
    wuv = wukv[:, :, MLA_NOPE:].reshape(kvl, N_HEADS * MLA_V).T
    return win.astype(BF16), wuq.astype(BF16), wuk.astype(BF16), wuv.astype(BF16)


def kernel(x, norm_mix, norm_ffn, norm_final, a_w_qkv, a_w_o, b_w_in, b_q_norm, b_w_uq, b_kv_norm, b_w_ukv, b_w_o,
           c_w_qkv, c_w_o, d_w_qkv, d_w_o, ffn_w_gate, ffn_conv_w, ffn_conv_b, ffn_w_up, ffn_w_down):
    b, s, d = x.shape
    m = b * s
    hd = N_HEADS * HEAD_DIM
    depth = norm_mix.shape[0]
    rope_h = _rope_tables(s, HEAD_DIM, 0, HEAD_DIM)
    rope_r = _rope_tables(s, MLA_ROPE, MLA_NOPE, MLA_QK_PAD)
    h = x.reshape(m, d)
    for i in range(depth):
        kind, j = i % 4, i // 4
        g = norm_mix[i]
        if kind == 0:
            w = a_w_qkv[j].astype(BF16)
            qscale = jnp.concatenate([jnp.full((hd,), MOBA_Q_SCALE, F32), jnp.ones((hd,), F32)])
            qk = _norm_matmul(h, g, w[:, :2 * hd], s, rope=rope_h, half=HEAD_DIM // 2, colscale=qscale)
            vt = _norm_matmul_t(h, g, w[:, 2 * hd:].T, b, s)
            qk = qk.reshape(b, s, 2 * hd)
            o = _flash_attention(qk, qk, vt, True, 0, N_HEADS // (2 * FLASH_PAIRS))
            h = _out_proj(o.reshape(m, hd), a_w_o[j].astype(BF16), h)
        elif kind == 1:
            win, wuq, wuk, wuv = _mla_weights(b_w_in[j], b_w_uq[j], b_w_ukv[j])
            q, k, vt = _mla_proj(h, g, win, b_q_norm[j], b_kv_norm[j], wuq, wuk, wuv, rope_r, b, s)
            o = _flash_attention(q.reshape(b, s, -1), k.reshape(b, s, -1), vt, False, 0, 0)
            h = _out_proj(o.reshape(m, hd), b_w_o[j].astype(BF16), h)
        elif kind == 2:
            w = c_w_qkv[j].astype(BF16)
            dils = tuple(dil for _, dil in DIL_GROUPS)
            outs = [_dil_attention(_dil_proj(h, g, w[:, gi * 3 * hd:(gi + 1) * 3 * hd], rope_h, b, s, dil))
                    for gi, dil in enumerate(dils)]
            w_o = c_w_o[j].astype(BF16).reshape(N_HEADS // 2, 2, HEAD_DIM, d)
            zero = jnp.zeros_like(w_o[:, 0])
            w_o = jnp.concatenate([w_o[:, 0], zero, zero, w_o[:, 1]], axis=1).reshape(N_HEADS * 2 * HEAD_DIM, d)
            h = _dil_out_proj(outs, dils, w_o, h, s)
        else:
            qkv = _norm_matmul(h, g, d_w_qkv[j].astype(BF16), s).reshape(b, s, 3 * hd)
            h = _out_proj(_sb_attention(qkv).reshape(m, hd), d_w_o[j].astype(BF16), h)
        h = _ffn(h, norm_ffn[i], ffn_w_gate[i].astype(BF16), ffn_w_up[i].astype(BF16), ffn_w_down[i].astype(BF16),
                 ffn_conv_w[i], ffn_conv_b[i], norm_final, i == depth - 1, s)
    return h.reshape(b, s, d)
```
